```python
import jax, jax.numpy as jnp
from jax import lax
import numpy as np

D_MODEL = 1024
BATCH = 8
SEQ = 2048
DEPTH = 4
DEC_BATCH = 128
DEC_SEQ = 8
PAST_LEN = 16384
PAGE_SIZE = 128

MIX_WIDTH = D_MODEL
GDN_HEADS = 4
GDN_DK = 128
GDN_DV = 128
GDN_QK = GDN_HEADS * GDN_DK
GDN_V = GDN_HEADS * GDN_DV
GDN_CONV = 4
GDN_CHUNK = 64
SC_WIDTH = MIX_WIDTH - GDN_V
SC_CONV = 3
MEM_LEN = 256
X_HEADS = 4
X_HEAD_DIM = D_MODEL // X_HEADS
D_FF = 2816
RMS_EPS = 1e-6
QKV_WIDTH = 2 * GDN_QK + GDN_V
OFF_Z = QKV_WIDTH
OFF_BETA = OFF_Z + GDN_V
OFF_A = OFF_BETA + GDN_HEADS
OFF_SC = OFF_A + GDN_HEADS
IN_COLS = OFF_SC + 3 * SC_WIDTH

kernel_name = "hymba_gdn_shortconv_macaron_memxattn_step"


def rmsnorm(x, g):
    xf = x.astype(jnp.float32)
    y = xf * lax.rsqrt(jnp.mean(xf * xf, axis=-1, keepdims=True) + RMS_EPS)
    return (y * g.astype(jnp.float32)).astype(x.dtype)


def l2norm(t):
    return t * lax.rsqrt(jnp.sum(t * t, axis=-1, keepdims=True) + 1e-6)


def swiglu(x, w_gu, w_down):
    gate, up = jnp.split(x @ w_gu, 2, axis=-1)
    return (jax.nn.silu(gate) * up) @ w_down


def causal_dwconv(u, buf, w):
    width = w.shape[0]
    ucat = jnp.concatenate([buf.astype(u.dtype), u], axis=1)
    out = lax.conv_general_dilated(ucat, w[:, None, :].astype(u.dtype), window_strides=(1,),
                                   padding='VALID', dimension_numbers=('NWC', 'WIO', 'NWC'),
                                   feature_group_count=u.shape[-1])
    return out, ucat[:, ucat.shape[1] - (width - 1):]


def gated_delta_chunked(q, k, v, g, beta, s0):
    bsz, L = q.shape[0], q.shape[1]
    C = min(GDN_CHUNK, L)
    n = -(-L // C)
    pad = n * C - L

    def to_chunks(t):
        t = jnp.pad(t, [(0, 0), (0, pad)] + [(0, 0)] * (t.ndim - 2))
        t = t.reshape((bsz, n, C) + t.shape[2:])
        return jnp.moveaxis(t, 2, 3)

    q, k, v, g, beta = (to_chunks(t) for t in (q, k, v, g, beta))
    gc = jnp.cumsum(g, axis=-1)
    tri = jnp.tril(jnp.ones((C, C), dtype=bool))
    decay = jnp.exp(jnp.where(tri, gc[..., :, None] - gc[..., None, :], -jnp.inf))
    kb = k * beta[..., None]
    vb = v * beta[..., None]
    eye = jnp.eye(C, dtype=jnp.float32)
    a_strict = jnp.einsum('bnhik,bnhjk->bnhij', kb, k) * decay * (1.0 - eye)
    t_inv = lax.linalg.triangular_solve(eye + a_strict, jnp.broadcast_to(eye, a_strict.shape),
                                        left_side=True, lower=True, unit_diagonal=True)
    u = jnp.einsum('bnhij,bnhjv->bnhiv', t_inv, vb)
    w = jnp.einsum('bnhij,bnhjk->bnhik', t_inv, kb * jnp.exp(gc)[..., None])
    qk = jnp.einsum('bnhik,bnhjk->bnhij', q, k) * decay
    q_dec = q * jnp.exp(gc)[..., None]
    g_last = gc[..., -1]
    k_tail = k * jnp.exp(g_last[..., None] - gc)[..., None]
    xs = tuple(jnp.moveaxis(t, 1, 0) for t in (u, w, qk, q_dec, k_tail, g_last))

    def step(S, xs_i):
        u_i, w_i, qk_i, qdec_i, ktail_i, glast_i = xs_i
        v_new = u_i - jnp.einsum('bhck,bhkv->bhcv', w_i, S)
        o = jnp.einsum('bhck,bhkv->bhcv', qdec_i, S) + jnp.einsum('bhij,bhjv->bhiv', qk_i, v_new)
        S = S * jnp.exp(glast_i)[..., None, None] + jnp.einsum('bhck,bhcv->bhkv', ktail_i, v_new)
        return S, o

    s_fin, o = lax.scan(step, s0, xs)
    o = jnp.transpose(o, (1, 0, 3, 2, 4)).reshape(bsz, n * C, GDN_HEADS, GDN_DV)[:, :L]
    return o, s_fin


def gdn_group(proj, qkv_buf, s0, conv_w, a_log, dt_bias, g_norm):
    bsz, L = proj.shape[0], proj.shape[1]
    qkv_c, new_buf = causal_dwconv(proj[..., :QKV_WIDTH], qkv_buf, conv_w)
    qkv_c = jax.nn.silu(qkv_c.astype(jnp.float32))
    q = qkv_c[..., :GDN_QK].reshape(bsz, L, GDN_HEADS, GDN_DK)
    k = qkv_c[..., GDN_QK:2 * GDN_QK].reshape(bsz, L, GDN_HEADS, GDN_DK)
    v = qkv_c[..., 2 * GDN_QK:].reshape(bsz, L, GDN_HEADS, GDN_DV)
    q = l2norm(q) * (GDN_DK ** -0.5)
    k = l2norm(k)
    beta = jax.nn.sigmoid(proj[..., OFF_BETA:OFF_A].astype(jnp.float32))
    g = -jnp.exp(a_log.astype(jnp.float32)) * jax.nn.softplus(
        proj[..., OFF_A:OFF_SC].astype(jnp.float32) + dt_bias.astype(jnp.float32))
    o, s_new = gated_delta_chunked(q, k, v, g, beta, s0.astype(jnp.float32))
    z = proj[..., OFF_Z:OFF_BETA].astype(jnp.float32).reshape(bsz, L, GDN_HEADS, GDN_DV)
    o = rmsnorm(o, g_norm) * jax.nn.silu(z)
    return o.reshape(bsz, L, GDN_V).astype(proj.dtype), new_buf, s_new


def short_conv_group(proj_sc, buf, w):
    b_gate, c_gate, h = jnp.split(proj_sc, 3, axis=-1)
    y, new_buf = causal_dwconv(c_gate * h, buf, w)
    return b_gate * y, new_buf


def mem_kv(mem, w_k, w_v):
    bsz = mem.shape[0]
    k = (mem @ w_k).reshape(bsz, MEM_LEN, X_HEADS, X_HEAD_DIM)
    v = (mem @ w_v).reshape(bsz, MEM_LEN, X_HEADS, X_HEAD_DIM)
    return k, v


def mem_attend(xn, k, v, w_q, w_o):
    bsz, L = xn.shape[0], xn.shape[1]
    q = (xn @ w_q).reshape(bsz, L, X_HEADS, X_HEAD_DIM)
    s = jnp.einsum('blhd,bmhd->bhlm', q, k.astype(q.dtype)).astype(jnp.float32) * (X_HEAD_DIM ** -0.5)
    p = jax.nn.softmax(s, axis=-1).astype(q.dtype)
    o = jnp.einsum('bhlm,bmhd->blhd', p, v.astype(q.dtype)).reshape(bsz, L, X_HEADS * X_HEAD_DIM)
    return o @ w_o


def trunk_layer(x, mk, mv, qkv_buf, sc_buf, s0, lw):
    x = x + 0.5 * swiglu(rmsnorm(x, lw['g_ffn1']), lw['w_ffn1_gu'], lw['w_ffn1_down'])
    proj = rmsnorm(x, lw['g_mix']) @ lw['w_in']
    o_gdn, qkv_buf, s_new = gdn_group(proj, qkv_buf, s0, lw['conv_qkv_w'], lw['a_log'],
                                      lw['dt_bias'], lw['g_gdn_out'])
    o_sc, sc_buf = short_conv_group(proj[..., OFF_SC:], sc_buf, lw['sconv_w'])
    x = x + jnp.concatenate([o_gdn, o_sc], axis=-1) @ lw['w_out']
    x = x + mem_attend(rmsnorm(x, lw['g_xattn']), mk, mv, lw['w_xq'], lw['w_xo'])
    x = x + 0.5 * swiglu(rmsnorm(x, lw['g_ffn2']), lw['w_ffn2_gu'], lw['w_ffn2_down'])
    return x, qkv_buf, sc_buf, s_new


def setup_inputs(seed: int = 0) -> dict:
    key = jax.random.key(seed)
    ks = jax.random.split(key, 32)
    f32 = jnp.float32

    def nrm(k, shape, scale):
        return jax.random.normal(k, shape, f32) * scale

    def gain(k, shape):
        return 1.0 + 0.02 * jax.random.normal(k, shape, f32)

    a_log = jnp.log(jax.random.uniform(ks[13], (DEPTH, GDN_HEADS), f32, 1.0, 16.0))
    dt = jnp.exp(jax.random.uniform(ks[14], (DEPTH, GDN_HEADS), f32, np.log(1e-3), np.log(1e-1)))
    dt_bias = dt + jnp.log(-jnp.expm1(-dt))
    return {
        "x_prompt": nrm(ks[0], (BATCH, SEQ, D_MODEL), 1.0),
        "x_sample": nrm(ks[1], (DEC_BATCH, DEC_SEQ, D_MODEL), 1.0),
        "mem_prompt": nrm(ks[2], (BATCH, MEM_LEN, D_MODEL), 1.0),
        "state_gdn": nrm(ks[3], (DEPTH, DEC_BATCH, GDN_HEADS, GDN_DK, GDN_DV), GDN_DK ** -0.5),
        "state_qkv_conv": nrm(ks[4], (DEPTH, DEC_BATCH, GDN_CONV - 1, QKV_WIDTH), 1.0),
        "state_short_conv": nrm(ks[5], (DEPTH, DEC_BATCH, SC_CONV - 1, SC_WIDTH), 1.0),
        "cache_mem_k": nrm(ks[6], (DEPTH, DEC_BATCH, MEM_LEN, X_HEADS, X_HEAD_DIM), 1.0),
        "cache_mem_v": nrm(ks[7], (DEPTH, DEC_BATCH, MEM_LEN, X_HEADS, X_HEAD_DIM), 1.0),
        "g_ffn1": gain(ks[8], (DEPTH, D_MODEL)),
        "w_ffn1_gu": nrm(ks[9], (DEPTH, D_MODEL, 2 * D_FF), D_MODEL ** -0.5),
        "w_ffn1_down": nrm(ks[10], (DEPTH, D_FF, D_MODEL), D_FF ** -0.5),
        "g_mix": gain(ks[11], (DEPTH, D_MODEL)),
        "w_in": nrm(ks[12], (DEPTH, D_MODEL, IN_COLS), D_MODEL ** -0.5),
        "conv_qkv_w": nrm(ks[15], (DEPTH, GDN_CONV, QKV_WIDTH), GDN_CONV ** -0.5),
        "a_log": a_log,
        "dt_bias": dt_bias,
        "g_gdn_out": gain(ks[16], (DEPTH, GDN_DV)),
        "sconv_w": nrm(ks[17], (DEPTH, SC_CONV, SC_WIDTH), SC_CONV ** -0.5),
        "w_out": nrm(ks[18], (DEPTH, MIX_WIDTH, D_MODEL), MIX_WIDTH ** -0.5),
        "g_xattn": gain(ks[19], (DEPTH, D_MODEL)),
        "w_xq": nrm(ks[20], (DEPTH, D_MODEL, X_HEADS * X_HEAD_DIM), D_MODEL ** -0.5),
        "w_xk": nrm(ks[21], (DEPTH, D_MODEL, X_HEADS * X_HEAD_DIM), D_MODEL ** -0.5),
        "w_xv": nrm(ks[22], (DEPTH, D_MODEL, X_HEADS * X_HEAD_DIM), D_MODEL ** -0.5),
        "w_xo": nrm(ks[23], (DEPTH, X_HEADS * X_HEAD_DIM, D_MODEL), D_MODEL ** -0.5),
        "g_ffn2": gain(ks[24], (DEPTH, D_MODEL)),
        "w_ffn2_gu": nrm(ks[25], (DEPTH, D_MODEL, 2 * D_FF), D_MODEL ** -0.5),
        "w_ffn2_down": nrm(ks[26], (DEPTH, D_FF, D_MODEL), D_FF ** -0.5),
        "g_final": gain(ks[27], (D_MODEL,)),
    }


def reference(x_prompt, x_sample, mem_prompt, state_gdn, state_qkv_conv, state_short_conv,
              cache_mem_k, cache_mem_v, g_ffn1, w_ffn1_gu, w_ffn1_down, g_mix, w_in, conv_qkv_w,
              a_log, dt_bias, g_gdn_out, sconv_w, w_out, g_xattn, w_xq, w_xk, w_xv, w_xo,
              g_ffn2, w_ffn2_gu, w_ffn2_down, g_final):
    yp, ys = x_prompt, x_sample
    p_s, p_qb, p_sb, p_mk, p_mv = [], [], [], [], []
    s_s, s_qb, s_sb = [], [], []
    for l in range(DEPTH):
        lw = {
            'g_ffn1': g_ffn1[l], 'w_ffn1_gu': w_ffn1_gu[l], 'w_ffn1_down': w_ffn1_down[l],
            'g_mix': g_mix[l], 'w_in': w_in[l], 'conv_qkv_w': conv_qkv_w[l], 'a_log': a_log[l],
            'dt_bias': dt_bias[l], 'g_gdn_out': g_gdn_out[l], 'sconv_w': sconv_w[l],
            'w_out': w_out[l], 'g_xattn': g_xattn[l], 'w_xq': w_xq[l], 'w_xo': w_xo[l],
            'g_ffn2': g_ffn2[l], 'w_ffn2_gu': w_ffn2_gu[l], 'w_ffn2_down': w_ffn2_down[l],
        }
        mk, mv = mem_kv(mem_prompt, w_xk[l], w_xv[l])
        yp, qb, sb, st = trunk_layer(
            yp, mk, mv,
            jnp.zeros((BATCH, GDN_CONV - 1, QKV_WIDTH), yp.dtype),
            jnp.zeros((BATCH, SC_CONV - 1, SC_WIDTH), yp.dtype),
            jnp.zeros((BATCH, GDN_HEADS, GDN_DK, GDN_DV), jnp.float32), lw)
        p_s.append(st); p_qb.append(qb); p_sb.append(sb); p_mk.append(mk); p_mv.append(mv)
        ys, qb, sb, st = trunk_layer(ys, cache_mem_k[l], cache_mem_v[l], state_qkv_conv[l],
                                     state_short_conv[l], state_gdn[l], lw)
        s_s.append(st); s_qb.append(qb); s_sb.append(sb)
    y_prompt = rmsnorm(yp, g_final)
    y_sample = rmsnorm(ys, g_final)
    return (y_prompt, y_sample,
            jnp.stack(p_s), jnp.stack(p_qb), jnp.stack(p_sb), jnp.stack(p_mk), jnp.stack(p_mv),
            jnp.stack(s_s), jnp.stack(s_qb), jnp.stack(s_sb))
```

```python
import functools

import jax
import jax.numpy as jnp
from jax import lax
from jax.experimental import pallas as pl
from jax.experimental.pallas import tpu as pltpu

F32 = jnp.float32
BF16 = jnp.bfloat16

GDN_HEADS = 4
GDN_DK = 128
GDN_DV = 128
GDN_QK = GDN_HEADS * GDN_DK
GDN_V = GDN_HEADS * GDN_DV
QKV_WIDTH = 2 * GDN_QK + GDN_V
GDN_CONV = 4
SC_CONV = 3
CHUNK = 64
X_HEADS = 4
RMS_EPS = 1e-6
LANES = 128
SUBLANES = 8
VMEM_LIMIT = 56 * 1024 * 1024

COL_Z = QKV_WIDTH
COL_SCB = COL_Z + GDN_V
GATE_BETA_LANE = 0
GATE_A_LANE = GDN_HEADS


def _params(sem):
    return pltpu.CompilerParams(dimension_semantics=sem, vmem_limit_bytes=VMEM_LIMIT)


def _pick_tile(n, pref):
    t = min(pref, n)
    while n % t:
        t -= SUBLANES
    assert t > 0 and t % SUBLANES == 0
    return t


def _silu(x):
    return x * jax.nn.sigmoid(x)


def _rms(x, g):
    return x * lax.rsqrt(jnp.mean(x * x, axis=-1, keepdims=True) + RMS_EPS) * g


def _dot(a, b):
    return jnp.dot(a.astype(BF16), b.astype(BF16), preferred_element_type=F32)


def _dot_nt(a, b):
    return lax.dot_general(a.astype(BF16), b.astype(BF16), (((1,), (1,)), ((), ())),
                           preferred_element_type=F32)


def _hdot(a, b):
    return jnp.dot(a, b, preferred_element_type=F32, precision=lax.Precision.HIGHEST)


def _ffn_kernel(x_ref, g_ref, wgu_ref, wd_ref, gf_ref, o_ref, h_ref, *, d_ff, col_chunk, final_norm):
    x = x_ref[...]
    xb = _rms(x, g_ref[...]).astype(BF16)
    for c in range(d_ff // col_chunk):
        lo = c * col_chunk
        gate = jnp.dot(xb, wgu_ref[:, lo:lo + col_chunk], preferred_element_type=F32)
        up = jnp.dot(xb, wgu_ref[:, d_ff + lo:d_ff + lo + col_chunk], preferred_element_type=F32)
        h_ref[:, lo:lo + col_chunk] = (_silu(gate) * up).astype(BF16)
    y = x + 0.5 * jnp.dot(h_ref[...], wd_ref[...], preferred_element_type=F32)
    if final_norm:
        y = _rms(y, gf_ref[...])
    o_ref[...] = y


def _ffn(x, g, wgu, wd, g_final, layer, *, final_norm):
    n, d = x.shape
    d_ff = wd.shape[1]
    tm = _pick_tile(n, 512)
    col_chunk = 256
    assert d_ff % col_chunk == 0
    kern = functools.partial(_ffn_kernel, d_ff=d_ff, col_chunk=col_chunk, final_norm=final_norm)
    return pl.pallas_call(
        kern,
        grid=(n // tm,),
        in_specs=[
            pl.BlockSpec((tm, d), lambda i: (i, 0)),
            pl.BlockSpec((None, 1, d), lambda i: (layer, 0, 0)),
            pl.BlockSpec((None, d, 2 * d_ff), lambda i: (layer, 0, 0)),
            pl.BlockSpec((None, d_ff, d), lambda i: (layer, 0, 0)),
            pl.BlockSpec((1, d), lambda i: (0, 0)),
        ],
        out_specs=pl.BlockSpec((tm, d), lambda i: (i, 0)),
        out_shape=jax.ShapeDtypeStruct((n, d), F32),
        scratch_shapes=[pltpu.VMEM((tm, d_ff), BF16)],
        compiler_params=_params(("arbitrary",)),
        name="ffn",
    )(x, g, wgu, wd, g_final)


def _mm_kernel(*refs, rms, residual):
    it = iter(refs)
    x_ref = next(it)
    g_ref = next(it) if rms else None
    w_ref = next(it)
    r_ref = next(it) if residual else None
    o_ref = next(it)
    x = x_ref[...]
    if rms:
        x = _rms(x, g_ref[...])
    acc = jnp.dot(x.astype(BF16), w_ref[...], preferred_element_type=F32)
    if residual:
        acc = acc + r_ref[...]
    o_ref[...] = acc.astype(o_ref.dtype)


def _matmul(x, w, layer, *, g=None, res=None, out_dtype=F32, tm_pref=512, name="mm"):
    n_rows, k = x.shape
    n_out = w.shape[2]
    tm = _pick_tile(n_rows, tm_pref)
    args = [x]
    specs = [pl.BlockSpec((tm, k), lambda i: (i, 0))]
    if g is not None:
        args.append(g)
        specs.append(pl.BlockSpec((None, 1, k), lambda i: (layer, 0, 0)))
    args.append(w)
    specs.append(pl.BlockSpec((None, k, n_out), lambda i: (layer, 0, 0)))
    if res is not None:
        args.append(res)
        specs.append(pl.BlockSpec((tm, n_out), lambda i: (i, 0)))
    kern = functools.partial(_mm_kernel, rms=g is not None, residual=res is not None)
    return pl.pallas_call(
        kern,
        grid=(n_rows // tm,),
        in_specs=specs,
        out_specs=pl.BlockSpec((tm, n_out), lambda i: (i, 0)),
        out_shape=jax.ShapeDtypeStruct((n_rows, n_out), out_dtype),
        compiler_params=_params(("arbitrary",)),
        name=name,
    )(*args)


def _iota2(shape, dim):
    return lax.broadcasted_iota(jnp.int32, shape, dim)


def _unit_lower_inverse(a, merge_levels):
    n = a.shape[0]
    r = _iota2((n, n), 0)
    c = _iota2((n, n), 1)
    eye = (r == c).astype(F32)
    d = jnp.where((r >> 3) == (c >> 3), a, 0.0)
    d2 = _hdot(d, d)
    d4 = _hdot(d2, d2)
    t = _hdot(_hdot(eye - d, eye + d2), eye + d4)
    for lvl in merge_levels:
        same_big = (r >> (lvl + 1)) == (c >> (lvl + 1))
        diff_small = (r >> lvl) != (c >> lvl)
        low = jnp.where(same_big & diff_small, a, 0.0)
        t = t - _hdot(t, _hdot(low, t))
    return t


def _l2n(t):
    return t * lax.rsqrt(jnp.sum(t * t, axis=-1, keepdims=True) + 1e-6)


def _segment_cumsum(g, seg):
    rin = _iota2(g.shape, 0) & (seg - 1)
    sh = 1
    while sh < seg:
        g = g + jnp.where(rin >= sh, pltpu.roll(g, sh, axis=0), 0.0)
        sh *= 2
    return g


def _segment_last(gc, seg):
    n = gc.shape[0]
    rin = _iota2(gc.shape, 0) & (seg - 1)
    x = jnp.where(rin == seg - 1, gc, 0.0)
    sh = 1
    while sh < seg:
        x = x + jnp.where(rin + sh <= seg - 1, pltpu.roll(x, n - sh, axis=0), 0.0)
        sh *= 2
    return x


def _gates(gate_blk, alog, dtb, seg):
    beta = jax.nn.sigmoid(gate_blk)
    xg = gate_blk + dtb
    softplus = jnp.maximum(xg, 0.0) + jnp.log1p(jnp.exp(-jnp.abs(xg)))
    g = -jnp.exp(alog) * softplus
    gc = _segment_cumsum(g, seg)
    return beta, gc


def _head_chunk_prep(qkv, beta, gc, gc_t, egc, ekt, rows, row0, h, seg):
    n = CHUNK
    lo = h * GDN_DK
    q = _l2n(qkv[rows, lo:lo + GDN_DK]) * (GDN_DK ** -0.5)
    k = _l2n(qkv[rows, GDN_QK + lo:GDN_QK + lo + GDN_DK])
    v = qkv[rows, 2 * GDN_QK + lo:2 * GDN_QK + lo + GDN_DV]
    bh = beta[rows, GATE_BETA_LANE + h:GATE_BETA_LANE + h + 1]
    al = GATE_A_LANE + h
    gcol = gc[rows, al:al + 1]
    grow = gc_t[al:al + 1, row0:row0 + n]
    r = _iota2((n, n), 0)
    c = _iota2((n, n), 1)
    lg = seg.bit_length() - 1
    tri = (r >= c) & ((r >> lg) == (c >> lg))
    strict = (r > c) & ((r >> lg) == (c >> lg))
    dec = jnp.where(tri, jnp.exp(jnp.where(tri, gcol - grow, 0.0)), 0.0)
    kb = k * bh
    a = jnp.where(strict, _dot_nt(kb, k) * dec, 0.0)
    tinv = _unit_lower_inverse(a, merge_levels=tuple(range(3, lg)))
    eg = egc[rows, al:al + 1]
    uw = _dot(tinv, jnp.concatenate([v * bh, kb * eg], axis=1))
    u = uw[:, :GDN_DV]
    w = uw[:, GDN_DV:]
    qk = _dot_nt(q, k) * dec
    qd = q * eg
    kt = k * ekt[rows, al:al + 1]
    return u, w, qk, qd, kt


def _gated_out_norm(o, z, gn):
    return _rms(o, gn) * _silu(z)


def _gdn_prompt_kernel(proj_ref, cw_ref, scw_ref, alog_ref, dtb_ref, gn_ref,
                       o_ref, st_ref, chb_ref,
                       s_scr, extq, extc, o_scr, *, lt):
    t = pl.program_id(1)
    nt = pl.num_programs(1)

    @pl.when(t == 0)
    def _():
        s_scr[...] = jnp.zeros_like(s_scr)
        extq[0:SUBLANES, :] = jnp.zeros((SUBLANES, QKV_WIDTH), F32)
        extc[0:SUBLANES, :] = jnp.zeros((SUBLANES, extc.shape[1]), F32)

    u_pre = proj_ref[:, 0:QKV_WIDTH]
    extq[SUBLANES:SUBLANES + lt, :] = u_pre
    conv = cw_ref[GDN_CONV - 1:GDN_CONV, :] * u_pre
    for s in range(1, GDN_CONV):
        conv = conv + cw_ref[GDN_CONV - 1 - s:GDN_CONV - s, :] * extq[SUBLANES - s:SUBLANES - s + lt, :]
    extq[0:SUBLANES, :] = u_pre[lt - SUBLANES:lt, :]
    qkv = _silu(conv)

    beta, gc = _gates(proj_ref[:, COL_SCB + 3 * GDN_V:COL_SCB + 3 * GDN_V + LANES],
                      alog_ref[...], dtb_ref[...], CHUNK)
    egc = jnp.exp(gc)
    glast = _segment_last(gc, CHUNK)
    ekt = jnp.exp(glast - gc)
    eglast = jnp.exp(glast)
    gc_t = gc.T

    for c in range(lt // CHUNK):
        row0 = c * CHUNK
        rows = slice(row0, row0 + CHUNK)
        for h in range(GDN_HEADS):
            u, w, qk, qd, kt = _head_chunk_prep(qkv, beta, gc, gc_t, egc, ekt, rows, row0, h, CHUNK)
            s_old = s_scr[h]
            ws = _dot(jnp.concatenate([w, qd], axis=0), s_old)
            v_new = u - ws[:CHUNK]
            o_scr[rows, h * GDN_DV:(h + 1) * GDN_DV] = ws[CHUNK:] + _dot(qk, v_new)
            al = GATE_A_LANE + h
            s_scr[h] = s_old * eglast[row0:row0 + 1, al:al + 1] + _dot(kt.T, v_new)

    z = proj_ref[:, COL_Z:COL_Z + GDN_V]
    for h in range(GDN_HEADS):
        hs = slice(h * GDN_DV, (h + 1) * GDN_DV)
        o_ref[:, hs] = _gated_out_norm(o_scr[:, hs], z[:, hs], gn_ref[...]).astype(o_ref.dtype)

    scw = extc.shape[1]
    b_gate = proj_ref[:, COL_SCB:COL_SCB + scw]
    ch = proj_ref[:, COL_SCB + scw:COL_SCB + 2 * scw] * proj_ref[:, COL_SCB + 2 * scw:COL_SCB + 3 * scw]
    extc[SUBLANES:SUBLANES + lt, :] = ch
    y = scw_ref[SC_CONV - 1:SC_CONV, :] * ch
    for s in range(1, SC_CONV):
        y = y + scw_ref[SC_CONV - 1 - s:SC_CONV - s, :] * extc[SUBLANES - s:SUBLANES - s + lt, :]
    extc[0:SUBLANES, :] = ch[lt - SUBLANES:lt, :]
    o_ref[:, GDN_V:GDN_V + scw] = (b_gate * y).astype(o_ref.dtype)

    @pl.when(t == nt - 1)
    def _():
        st_ref[...] = s_scr[...]
        chb_ref[...] = ch[lt - SUBLANES:lt, :]


def _gdn_prompt(proj, conv_w, sconv_w, alog_v, dtb_v, g_norm, layer, *, batch, seq, n_rows_out, d_model):
    lt = _pick_tile(seq, 128)
    assert lt % CHUNK == 0
    nt = seq // lt
    cols = proj.shape[1]
    scw = sconv_w.shape[2]
    kern = functools.partial(_gdn_prompt_kernel, lt=lt)
    return pl.pallas_call(
        kern,
        grid=(batch, nt),
        in_specs=[
            pl.BlockSpec((lt, cols), lambda b, t: (b * nt + t, 0)),
            pl.BlockSpec((None, GDN_CONV, QKV_WIDTH), lambda b, t: (layer, 0, 0)),
            pl.BlockSpec((None, SC_CONV, scw), lambda b, t: (layer, 0, 0)),
            pl.BlockSpec((None, 1, LANES), lambda b, t: (layer, 0, 0)),
            pl.BlockSpec((None, 1, LANES), lambda b, t: (layer, 0, 0)),
            pl.BlockSpec((None, 1, GDN_DV), lambda b, t: (layer, 0, 0)),
        ],
        out_specs=[
            pl.BlockSpec((lt, d_model), lambda b, t: (b * nt + t, 0)),
            pl.BlockSpec((None, GDN_HEADS, GDN_DK, GDN_DV), lambda b, t: (b, 0, 0, 0)),
            pl.BlockSpec((None, SUBLANES, scw), lambda b, t: (b, 0, 0)),
        ],
        out_shape=[
            jax.ShapeDtypeStruct((n_rows_out, d_model), BF16),
            jax.ShapeDtypeStruct((batch, GDN_HEADS, GDN_DK, GDN_DV), F32),
            jax.ShapeDtypeStruct((batch, SUBLANES, scw), F32),
        ],
        scratch_shapes=[
            pltpu.VMEM((GDN_HEADS, GDN_DK, GDN_DV), F32),
            pltpu.VMEM((lt + SUBLANES, QKV_WIDTH), F32),
            pltpu.VMEM((lt + SUBLANES, scw), F32),
            pltpu.VMEM((lt, GDN_V), F32),
        ],
        compiler_params=_params(("arbitrary", "arbitrary")),
        name="gdn_prompt",
    )(proj, conv_w, sconv_w, alog_v, dtb_v, g_norm)


def _shift_in_segments(x, buf, s, buf_rows, seg):
    n = x.shape[0]
    rin = _iota2(x.shape, 0) & (seg - 1)
    from_x = pltpu.roll(x, s, axis=0)
    back = (n + s - buf_rows) % n
    from_buf = pltpu.roll(buf, back, axis=0) if back else buf
    return jnp.where(rin >= s, from_x, from_buf)


def _gdn_sample_kernel(mix_in_ref, proj_ref, st_in_ref, qbuf_ref, cbuf_ref, cw_ref, scw_ref,
                       alog_ref, dtb_ref, gn_ref,
                       o_ref, st_ref, ch_ref, *, seq):
    del mix_in_ref
    n = CHUNK
    nseq = n // seq
    u_pre = proj_ref[:, 0:QKV_WIDTH]
    qbuf = qbuf_ref[...]
    conv = cw_ref[GDN_CONV - 1:GDN_CONV, :] * u_pre
    for s in range(1, GDN_CONV):
        conv = conv + cw_ref[GDN_CONV - 1 - s:GDN_CONV - s, :] * _shift_in_segments(
            u_pre, qbuf, s, GDN_CONV - 1, seq)
    qkv = _silu(conv)

    beta, gc = _gates(proj_ref[:, COL_SCB + 3 * GDN_V:COL_SCB + 3 * GDN_V + LANES],
                      alog_ref[...], dtb_ref[...], seq)
    egc = jnp.exp(gc)
    glast = _segment_last(gc, seq)
    ekt = jnp.exp(glast - gc)
    eglast = jnp.exp(glast)
    gc_t = gc.T
    rows = slice(0, n)
    rseq = _iota2((n, GDN_DV), 0) // seq

    for h in range(GDN_HEADS):
        u, w, qk, qd, kt = _head_chunk_prep(qkv, beta, gc, gc_t, egc, ekt, rows, 0, h, seq)
        kt_t = kt.T
        al = GATE_A_LANE + h
        v_parts = []
        o_parts = []
        s_olds = []
        for s in range(nseq):
            rs = slice(s * seq, (s + 1) * seq)
            s_old = st_in_ref[s, h]
            ws = _dot(jnp.concatenate([w[rs], qd[rs]], axis=0), s_old)
            v_parts.append(u[rs] - ws[:seq])
            o_parts.append(ws[seq:])
            s_olds.append(s_old)
        v_new = jnp.concatenate(v_parts, axis=0)
        o = jnp.concatenate(o_parts, axis=0) + _dot(qk, v_new)
        for s in range(nseq):
            v_s = jnp.where(rseq == s, v_new, 0.0)
            st_ref[s, h] = (s_olds[s] * eglast[s * seq:s * seq + 1, al:al + 1] + _dot(kt_t, v_s))
        hs = slice(h * GDN_DV, (h + 1) * GDN_DV)
        z = proj_ref[:, COL_Z + h * GDN_DV:COL_Z + (h + 1) * GDN_DV]
        o_ref[:, hs] = _gated_out_norm(o, z, gn_ref[...]).astype(o_ref.dtype)

    scw = cbuf_ref.shape[1]
    b_gate = proj_ref[:, COL_SCB:COL_SCB + scw]
    ch = proj_ref[:, COL_SCB + scw:COL_SCB + 2 * scw] * proj_ref[:, COL_SCB + 2 * scw:COL_SCB + 3 * scw]
    cbuf = cbuf_ref[...]
    y = scw_ref[SC_CONV - 1:SC_CONV, :] * ch
    for s in range(1, SC_CONV):
        y = y + scw_ref[SC_CONV - 1 - s:SC_CONV - s, :] * _shift_in_segments(ch, cbuf, s, SC_CONV - 1, seq)
    o_ref[:, GDN_V:GDN_V + scw] = (b_gate * y).astype(o_ref.dtype)
    ch_ref[...] = ch


def _gdn_sample(mix, proj, state, qbuf, cbuf, conv_w, sconv_w, alog_v, dtb_v, g_norm, layer, *,
                row0, n_seq, seq, d_model):
    assert seq == SUBLANES and CHUNK % seq == 0
    spb = CHUNK // seq
    assert n_seq % spb == 0 and row0 % CHUNK == 0
    blk0 = row0 // CHUNK
    cols = proj.shape[1]
    scw = sconv_w.shape[2]
    kern = functools.partial(_gdn_sample_kernel, seq=seq)
    return pl.pallas_call(
        kern,
        grid=(n_seq // spb,),
        in_specs=[
            pl.BlockSpec(memory_space=pl.ANY),
            pl.BlockSpec((CHUNK, cols), lambda i: (blk0 + i, 0)),
            pl.BlockSpec((None, spb, GDN_HEADS, GDN_DK, GDN_DV), lambda i: (layer, i, 0, 0, 0)),
            pl.BlockSpec((CHUNK, QKV_WIDTH), lambda i: (i, 0)),
            pl.BlockSpec((CHUNK, scw), lambda i: (i, 0)),
            pl.BlockSpec((None, GDN_CONV, QKV_WIDTH), lambda i: (layer, 0, 0)),
            pl.BlockSpec((None, SC_CONV, scw), lambda i: (layer, 0, 0)),
            pl.BlockSpec((None, 1, LANES), lambda i: (layer, 0, 0)),
            pl.BlockSpec((None, 1, LANES), lambda i: (layer, 0, 0)),
            pl.BlockSpec((None, 1, GDN_DV), lambda i: (layer, 0, 0)),
        ],
        out_specs=[
            pl.BlockSpec((CHUNK, d_model), lambda i: (blk0 + i, 0)),
            pl.BlockSpec((spb, GDN_HEADS, GDN_DK, GDN_DV), lambda i: (i, 0, 0, 0)),
            pl.BlockSpec((CHUNK, scw), lambda i: (i, 0)),
        ],
        out_shape=[
            jax.ShapeDtypeStruct(mix.shape, mix.dtype),
            jax.ShapeDtypeStruct((n_seq, GDN_HEADS, GDN_DK, GDN_DV), F32),
            jax.ShapeDtypeStruct((n_seq * seq, scw), F32),
        ],
        input_output_aliases={0: 0},
        compiler_params=_params(("arbitrary",)),
        name="gdn_sample",
    )(mix, proj, state, qbuf, cbuf, conv_w, sconv_w, alog_v, dtb_v, g_norm)


def _attend_head(q, k, v):
    hd = q.shape[-1]
    s = _dot_nt(q, k) * (hd ** -0.5)
    m = jnp.max(s, axis=-1, keepdims=True)
    p = jnp.exp(s - m)
    p = p / jnp.sum(p, axis=-1, keepdims=True)
    return _dot(p, v)


def _attn_prompt_kernel(q_ref, k_ref, v_ref, o_ref):
    hd = q_ref.shape[1] // X_HEADS
    for h in range(X_HEADS):
        hs = slice(h * hd, (h + 1) * hd)
        o_ref[:, hs] = _attend_head(q_ref[:, hs], k_ref[:, hs], v_ref[:, hs]).astype(o_ref.dtype)


def _attn_prompt(q, mk, mv, *, batch, seq, mem_len):
    d = q.shape[1]
    tq = _pick_tile(seq, 512)
    nq = seq // tq
    return pl.pallas_call(
        _attn_prompt_kernel,
        grid=(batch, nq),
        in_specs=[
            pl.BlockSpec((tq, d), lambda b, t: (b * nq + t, 0)),
            pl.BlockSpec((mem_len, d), lambda b, t: (b, 0)),
            pl.BlockSpec((mem_len, d), lambda b, t: (b, 0)),
        ],
        out_specs=pl.BlockSpec((tq, d), lambda b, t: (b * nq + t, 0)),
        out_shape=jax.ShapeDtypeStruct(q.shape, BF16),
        compiler_params=_params(("arbitrary", "arbitrary")),
        name="attn_prompt",
    )(q, mk, mv)


def _attn_sample_kernel(att_in_ref, q_ref, k_ref, v_ref, o_ref, *, seq, spb):
    del att_in_ref
    hd = q_ref.shape[1] // X_HEADS
    for s in range(spb):
        rs = slice(s * seq, (s + 1) * seq)
        for h in range(X_HEADS):
            hs = slice(h * hd, (h + 1) * hd)
            o_ref[rs, hs] = _attend_head(q_ref[rs, hs], k_ref[s, :, hs], v_ref[s, :, hs]).astype(o_ref.dtype)


def _attn_sample(att, q, cache_k, cache_v, layer, *, row0, n_seq, seq, mem_len):
    d = q.shape[1]
    spb = 4
    assert n_seq % spb == 0 and row0 % (spb * seq) == 0
    rows = spb * seq
    blk0 = row0 // rows
    kern = functools.partial(_attn_sample_kernel, seq=seq, spb=spb)
    return pl.pallas_call(
        kern,
        grid=(n_seq // spb,),
        in_specs=[
            pl.BlockSpec(memory_space=pl.ANY),
            pl.BlockSpec((rows, d), lambda i: (blk0 + i, 0)),
            pl.BlockSpec((None, spb, mem_len, d), lambda i: (layer, i, 0, 0)),
            pl.BlockSpec((None, spb, mem_len, d), lambda i: (layer, i, 0, 0)),
        ],
        out_specs=pl.BlockSpec((rows, d), lambda i: (blk0 + i, 0)),
        out_shape=jax.ShapeDtypeStruct(att.shape, att.dtype),
        input_output_aliases={0: 0},
        compiler_params=_params(("arbitrary",)),
        name="attn_sample",
    )(att, q, cache_k, cache_v)


def _gate_lane_vector(v, lane0):
    depth, n = v.shape
    out = jnp.zeros((depth, 1, LANES), F32)
    return out.at[:, 0, lane0:lane0 + n].set(v.astype(F32))


def kernel(x_prompt, x_sample, mem_prompt, state_gdn, state_qkv_conv, state_short_conv, cache_mem_k, cache_mem_v, g_ffn1, w_ffn1_gu, w_ffn1_down, g_mix, w_in, conv_qkv_w, a_log, dt_bias, g_gdn_out, sconv_w, w_out, g_xattn, w_xq, w_xk, w_xv, w_xo, g_ffn2, w_ffn2_gu, w_ffn2_down, g_final):
    batch, seq, d = x_prompt.shape
    dec_batch, dec_seq, _ = x_sample.shape
    depth = w_in.shape[0]
    mem_len = mem_prompt.shape[1]
    scw = sconv_w.shape[2]
    n_p = batch * seq
    n_s = dec_batch * dec_seq
    n_all = n_p + n_s
    off_beta = QKV_WIDTH + GDN_V
    off_sc = off_beta + 2 * GDN_HEADS

    w_in_p = jnp.concatenate(
        [w_in[:, :, :off_beta], w_in[:, :, off_sc:], w_in[:, :, off_beta:off_sc],
         jnp.zeros((depth, d, LANES - 2 * GDN_HEADS), w_in.dtype)], axis=2).astype(BF16)
    wgu1, wd1 = w_ffn1_gu.astype(BF16), w_ffn1_down.astype(BF16)
    wgu2, wd2 = w_ffn2_gu.astype(BF16), w_ffn2_down.astype(BF16)
    w_out_b, w_xq_b, w_xo_b = w_out.astype(BF16), w_xq.astype(BF16), w_xo.astype(BF16)
    w_xk_b, w_xv_b = w_xk.astype(BF16), w_xv.astype(BF16)
    alog_v = _gate_lane_vector(a_log, GATE_A_LANE)
    dtb_v = _gate_lane_vector(dt_bias, GATE_A_LANE)
    g1, gm, gx, g2 = (t.reshape(depth, 1, d) for t in (g_ffn1, g_mix, g_xattn, g_ffn2))
    gn = g_gdn_out.reshape(depth, 1, GDN_DV)
    gf = g_final.reshape(1, d)
    cache_k = cache_mem_k.reshape(depth, dec_batch, mem_len, d)
    cache_v = cache_mem_v.reshape(depth, dec_batch, mem_len, d)
    mem2 = mem_prompt.reshape(batch * mem_len, d)
    qbuf_all = jnp.pad(state_qkv_conv, ((0, 0), (0, 0), (0, dec_seq - (GDN_CONV - 1)), (0, 0)))
    cbuf_all = jnp.pad(state_short_conv, ((0, 0), (0, 0), (0, dec_seq - (SC_CONV - 1)), (0, 0)))

    x = jnp.concatenate([x_prompt.reshape(n_p, d), x_sample.reshape(n_s, d)], axis=0)

    p_s, p_qb, p_sb, p_mk, p_mv, s_s, s_qb, s_sb = [], [], [], [], [], [], [], []
    for l in range(depth):
        x = _ffn(x, g1, wgu1, wd1, gf, l, final_norm=False)
        proj = _matmul(x, w_in_p, l, g=gm, name="in_proj")

        mix, st_p, chb_p = _gdn_prompt(proj, conv_qkv_w, sconv_w, alog_v, dtb_v, gn, l,
                                       batch=batch, seq=seq, n_rows_out=n_all, d_model=d)
        mix, st_s, ch_s = _gdn_sample(mix, proj, state_gdn, qbuf_all[l].reshape(n_s, QKV_WIDTH),
                                      cbuf_all[l].reshape(n_s, scw), conv_qkv_w, sconv_w,
                                      alog_v, dtb_v, gn, l,
                                      row0=n_p, n_seq=dec_batch, seq=dec_seq, d_model=d)
        x = _matmul(mix, w_out_b, l, res=x, name="out_proj")

        mk = _matmul(mem2, w_xk_b, l, name="mem_k")
        mv = _matmul(mem2, w_xv_b, l, name="mem_v")
        q = _matmul(x, w_xq_b, l, g=gx, out_dtype=BF16, name="xq_proj")
        att = _attn_prompt(q, mk, mv, batch=batch, seq=seq, mem_len=mem_len)
        att = _attn_sample(att, q, cache_k, cache_v, l, row0=n_p, n_seq=dec_batch, seq=dec_seq,
                           mem_len=mem_len)
        x = _matmul(att, w_xo_b, l, res=x, name="xo_proj")

        x = _ffn(x, g2, wgu2, wd2, gf, l, final_norm=(l == depth - 1))

        p_s.append(st_p)
        p_qb.append(proj[:n_p].reshape(batch, seq, -1)[:, seq - (GDN_CONV - 1):, :QKV_WIDTH])
        p_sb.append(chb_p[:, SUBLANES - (SC_CONV - 1):, :])
        p_mk.append(mk.reshape(batch, mem_len, X_HEADS, d // X_HEADS))
        p_mv.append(mv.reshape(batch, mem_len, X_HEADS, d // X_HEADS))
        s_s.append(st_s)
        s_qb.append(proj[n_p:].reshape(dec_batch, dec_seq, -1)[:, dec_seq - (GDN_CONV - 1):, :QKV_WIDTH])
        s_sb.append(ch_s.reshape(dec_batch, dec_seq, scw)[:, dec_seq - (SC_CONV - 1):, :])

    y_prompt = x[:n_p].reshape(batch, seq, d)
    y_sample = x[n_p:].reshape(dec_batch, dec_seq, d)
    return (y_prompt, y_sample,
            jnp.stack(p_s), jnp.stack(p_qb), jnp.stack(p_sb), jnp.stack(p_mk), jnp.stack(p_mv),
            jnp.stack(s_s), jnp.stack(s_qb), jnp.stack(s_sb))
```

```python
import functools

import jax
import jax.numpy as jnp
from jax import lax
from jax.experimental import pallas as pl
from jax.experimental.pallas import tpu as pltpu

F32 = jnp.float32
BF16 = jnp.bfloat16

GDN_HEADS = 4
GDN_DK = 128
GDN_DV = 128
GDN_QK = GDN_HEADS * GDN_DK
GDN_V = GDN_HEADS * GDN_DV
QKV_WIDTH = 2 * GDN_QK + GDN_V
GDN_CONV = 4
SC_CONV = 3
CHUNK = 64
MAT_TILE = 128
X_HEADS = 4
RMS_EPS = 1e-6
LANES = 128
SUBLANES = 8
VMEM_LIMIT = 56 * 1024 * 1024

COL_Z = QKV_WIDTH
COL_SCB = COL_Z + GDN_V
GATE_BETA_LANE = 0
GATE_A_LANE = GDN_HEADS


def _params(sem):
    return pltpu.CompilerParams(dimension_semantics=sem, vmem_limit_bytes=VMEM_LIMIT)


def _pick_tile(n, pref):
    t = min(pref, n)
    while n % t:
        t -= SUBLANES
    assert t > 0 and t % SUBLANES == 0
    return t


def _silu(x):
    return x * jax.nn.sigmoid(x)


def _rms(x, g):
    return x * lax.rsqrt(jnp.mean(x * x, axis=-1, keepdims=True) + RMS_EPS) * g


def _dot(a, b):
    return jnp.dot(a.astype(BF16), b.astype(BF16), preferred_element_type=F32)


def _dot_nt(a, b):
    return lax.dot_general(a.astype(BF16), b.astype(BF16), (((1,), (1,)), ((), ())),
                           preferred_element_type=F32)


def _ffn_kernel(x_ref, g_ref, wgu_ref, wd_ref, gf_ref, o_ref, h_ref, *, d_ff, col_chunk, final_norm):
    x = x_ref[...]
    xb = _rms(x, g_ref[...]).astype(BF16)
    for c in range(d_ff // col_chunk):
        lo = c * col_chunk
        gate = jnp.dot(xb, wgu_ref[:, lo:lo + col_chunk], preferred_element_type=F32)
        up = jnp.dot(xb, wgu_ref[:, d_ff + lo:d_ff + lo + col_chunk], preferred_element_type=F32)
        h_ref[:, lo:lo + col_chunk] = (_silu(gate) * up).astype(BF16)
    y = x + 0.5 * jnp.dot(h_ref[...], wd_ref[...], preferred_element_type=F32)
    if final_norm:
        y = _rms(y, gf_ref[...])
    o_ref[...] = y


def _ffn(x, g, wgu, wd, g_final, layer, *, final_norm):
    n, d = x.shape
    d_ff = wd.shape[1]
    tm = _pick_tile(n, 512)
    col_chunk = 256
    assert d_ff % col_chunk == 0
    kern = functools.partial(_ffn_kernel, d_ff=d_ff, col_chunk=col_chunk, final_norm=final_norm)
    return pl.pallas_call(
        kern,
        grid=(n // tm,),
        in_specs=[
            pl.BlockSpec((tm, d), lambda i: (i, 0)),
            pl.BlockSpec((None, 1, d), lambda i: (layer, 0, 0)),
            pl.BlockSpec((None, d, 2 * d_ff), lambda i: (layer, 0, 0)),
            pl.BlockSpec((None, d_ff, d), lambda i: (layer, 0, 0)),
            pl.BlockSpec((1, d), lambda i: (0, 0)),
        ],
        out_specs=pl.BlockSpec((tm, d), lambda i: (i, 0)),
        out_shape=jax.ShapeDtypeStruct((n, d), F32),
        scratch_shapes=[pltpu.VMEM((tm, d_ff), BF16)],
        compiler_params=_params(("arbitrary",)),
        name="ffn",
    )(x, g, wgu, wd, g_final)


def _mm_kernel(*refs, rms, residual):
    it = iter(refs)
    x_ref = next(it)
    g_ref = next(it) if rms else None
    w_ref = next(it)
    r_ref = next(it) if residual else None
    o_ref = next(it)
    x = x_ref[...]
    if rms:
        x = _rms(x, g_ref[...])
    acc = jnp.dot(x.astype(BF16), w_ref[...], preferred_element_type=F32)
    if residual:
        acc = acc + r_ref[...]
    o_ref[...] = acc.astype(o_ref.dtype)


def _matmul(x, w, layer, *, g=None, res=None, out_dtype=F32, tm_pref=512, name="mm"):
    n_rows, k = x.shape
    n_out = w.shape[2]
    tm = _pick_tile(n_rows, tm_pref)
    args = [x]
    specs = [pl.BlockSpec((tm, k), lambda i: (i, 0))]
    if g is not None:
        args.append(g)
        specs.append(pl.BlockSpec((None, 1, k), lambda i: (layer, 0, 0)))
    args.append(w)
    specs.append(pl.BlockSpec((None, k, n_out), lambda i: (layer, 0, 0)))
    if res is not None:
        args.append(res)
        specs.append(pl.BlockSpec((tm, n_out), lambda i: (i, 0)))
    kern = functools.partial(_mm_kernel, rms=g is not None, residual=res is not None)
    return pl.pallas_call(
        kern,
        grid=(n_rows // tm,),
        in_specs=specs,
        out_specs=pl.BlockSpec((tm, n_out), lambda i: (i, 0)),
        out_shape=jax.ShapeDtypeStruct((n_rows, n_out), out_dtype),
        compiler_params=_params(("arbitrary",)),
        name=name,
    )(*args)


def _iota2(shape, dim):
    return lax.broadcasted_iota(jnp.int32, shape, dim)


def _unit_lower_inverse(a_list, merge_levels):
    n = a_list[0].shape[0]
    r = _iota2((n, n), 0)
    c = _iota2((n, n), 1)
    eye = (r == c).astype(F32)
    blk8 = (r >> 3) == (c >> 3)
    d = [jnp.where(blk8, a, 0.0) for a in a_list]
    d2 = [_dot(x, x) for x in d]
    p = [_dot(eye - x, eye + y) for x, y in zip(d, d2)]
    d4 = [_dot(y, y) for y in d2]
    t = [_dot(x, eye + y) for x, y in zip(p, d4)]
    for lvl in merge_levels:
        mask = ((r >> (lvl + 1)) == (c >> (lvl + 1))) & ((r >> lvl) != (c >> lvl))
        lt = [_dot(jnp.where(mask, a, 0.0), x) for a, x in zip(a_list, t)]
        t = [x - _dot(x, y) for x, y in zip(t, lt)]
    return t


def _l2n(t):
    return t * lax.rsqrt(jnp.sum(t * t, axis=-1, keepdims=True) + 1e-6)


def _segment_cumsum(g, seg):
    rin = _iota2(g.shape, 0) & (seg - 1)
    sh = 1
    while sh < seg:
        g = g + jnp.where(rin >= sh, pltpu.roll(g, sh, axis=0), 0.0)
        sh *= 2
    return g


def _segment_last(gc, seg):
    n = gc.shape[0]
    rin = _iota2(gc.shape, 0) & (seg - 1)
    x = jnp.where(rin == seg - 1, gc, 0.0)
    sh = 1
    while sh < seg:
        x = x + jnp.where(rin + sh <= seg - 1, pltpu.roll(x, n - sh, axis=0), 0.0)
        sh *= 2
    return x


def _gates(gate_blk, alog, dtb, seg):
    beta = jax.nn.sigmoid(gate_blk)
    xg = gate_blk + dtb
    softplus = jnp.maximum(xg, 0.0) + jnp.log1p(jnp.exp(-jnp.abs(xg)))
    g = -jnp.exp(alog) * softplus
    gc = _segment_cumsum(g, seg)
    return beta, gc


def _heads_prep(qkv, gate_blk, alog, dtb, seg):
    n = qkv.shape[0]
    mt = min(n, MAT_TILE)
    assert n % mt == 0 and mt % seg == 0
    tiles = [slice(j * mt, (j + 1) * mt) for j in range(n // mt)]
    lg = seg.bit_length() - 1
    beta, gc = _gates(gate_blk, alog, dtb, seg)
    egc = jnp.exp(gc)
    glast = _segment_last(gc, seg)
    ekt = jnp.exp(glast - gc)
    eglast = jnp.exp(glast)
    gc_t = gc.T
    r = _iota2((mt, mt), 0)
    c = _iota2((mt, mt), 1)
    same = (r >> lg) == (c >> lg)
    tri = (r >= c) & same
    strict = (r > c) & same
    heads = range(GDN_HEADS)
    q = [_l2n(qkv[:, h * GDN_DK:(h + 1) * GDN_DK]) * (GDN_DK ** -0.5) for h in heads]
    k = [_l2n(qkv[:, GDN_QK + h * GDN_DK:GDN_QK + (h + 1) * GDN_DK]) for h in heads]
    v = [qkv[:, 2 * GDN_QK + h * GDN_DV:2 * GDN_QK + (h + 1) * GDN_DV] for h in heads]
    bh = [beta[:, GATE_BETA_LANE + h:GATE_BETA_LANE + h + 1] for h in heads]
    eg = [egc[:, GATE_A_LANE + h:GATE_A_LANE + h + 1] for h in heads]
    kb = [k[h] * bh[h] for h in heads]
    rhs = [jnp.concatenate([v[h] * bh[h], kb[h] * eg[h]], axis=1) for h in heads]
    dec = [[None] * len(tiles) for _ in heads]
    for h in heads:
        al = GATE_A_LANE + h
        for j, rt in enumerate(tiles):
            diff = gc[rt, al:al + 1] - gc_t[al:al + 1, rt]
            dec[h][j] = jnp.where(tri, jnp.exp(jnp.where(tri, diff, 0.0)), 0.0)
    systems = [(h, j) for h in heads for j in range(len(tiles))]
    a = [jnp.where(strict, _dot_nt(kb[h][tiles[j]], k[h][tiles[j]]) * dec[h][j], 0.0)
         for h, j in systems]
    tinv = _unit_lower_inverse(a, merge_levels=tuple(range(3, lg)))
    uw = [_dot(t, rhs[h][tiles[j]]) for t, (h, j) in zip(tinv, systems)]
    uw = [jnp.concatenate(uw[h * len(tiles):(h + 1) * len(tiles)], axis=0) for h in heads]
    u = [x[:, :GDN_DV] for x in uw]
    w = [x[:, GDN_DV:] for x in uw]
    qk = [[_dot_nt(q[h][rt], k[h][rt]) * dec[h][j] for j, rt in enumerate(tiles)] for h in heads]
    qd = [q[h] * eg[h] for h in heads]
    kt = [k[h] * ekt[:, GATE_A_LANE + h:GATE_A_LANE + h + 1] for h in heads]
    return u, w, qk, qd, kt, eglast, tiles


def _gated_out_norm(o, z, gn):
    return _rms(o, gn) * _silu(z)


def _gdn_prompt_kernel(proj_ref, cw_ref, scw_ref, alog_ref, dtb_ref, gn_ref,
                       o_ref, st_ref, qb_ref, chb_ref,
                       s_scr, extq, extc, *, lt):
    t = pl.program_id(1)
    nt = pl.num_programs(1)
    heads = range(GDN_HEADS)

    @pl.when(t == 0)
    def _():
        s_scr[...] = jnp.zeros_like(s_scr)
        extq[0:SUBLANES, :] = jnp.zeros((SUBLANES, QKV_WIDTH), F32)
        extc[0:SUBLANES, :] = jnp.zeros((SUBLANES, extc.shape[1]), F32)

    u_pre = proj_ref[:, 0:QKV_WIDTH]
    extq[SUBLANES:SUBLANES + lt, :] = u_pre
    conv = cw_ref[GDN_CONV - 1:GDN_CONV, :] * u_pre
    for s in range(1, GDN_CONV):
        conv = conv + cw_ref[GDN_CONV - 1 - s:GDN_CONV - s, :] * extq[SUBLANES - s:SUBLANES - s + lt, :]
    extq[0:SUBLANES, :] = u_pre[lt - SUBLANES:lt, :]
    qkv = _silu(conv)

    u, w, qk, qd, kt, eglast, tiles = _heads_prep(
        qkv, proj_ref[:, COL_SCB + 3 * GDN_V:COL_SCB + 3 * GDN_V + LANES],
        alog_ref[...], dtb_ref[...], CHUNK)

    s_cur = [s_scr[h] for h in heads]
    v_parts = [[] for _ in heads]
    o_parts = [[] for _ in heads]
    for c in range(lt // CHUNK):
        row0 = c * CHUNK
        rs = slice(row0, row0 + CHUNK)
        ws = [_dot(jnp.concatenate([w[h][rs], qd[h][rs]], axis=0), s_cur[h]) for h in heads]
        vn = [u[h][rs] - ws[h][:CHUNK] for h in heads]
        s_cur = [s_cur[h] * eglast[row0:row0 + 1, GATE_A_LANE + h:GATE_A_LANE + h + 1]
                 + _dot(kt[h][rs].T, vn[h]) for h in heads]
        for h in heads:
            v_parts[h].append(vn[h])
            o_parts[h].append(ws[h][CHUNK:])
    z = proj_ref[:, COL_Z:COL_Z + GDN_V]
    for h in heads:
        s_scr[h] = s_cur[h]
        v_new = jnp.concatenate(v_parts[h], axis=0)
        o = jnp.concatenate(o_parts[h], axis=0) + jnp.concatenate(
            [_dot(qk[h][j], v_new[rt]) for j, rt in enumerate(tiles)], axis=0)
        hs = slice(h * GDN_DV, (h + 1) * GDN_DV)
        o_ref[:, hs] = _gated_out_norm(o, z[:, hs], gn_ref[...]).astype(o_ref.dtype)

    scw = extc.shape[1]
    b_gate = proj_ref[:, COL_SCB:COL_SCB + scw]
    ch = proj_ref[:, COL_SCB + scw:COL_SCB + 2 * scw] * proj_ref[:, COL_SCB + 2 * scw:COL_SCB + 3 * scw]
    extc[SUBLANES:SUBLANES + lt, :] = ch
    y = scw_ref[SC_CONV - 1:SC_CONV, :] * ch
    for s in range(1, SC_CONV):
        y = y + scw_ref[SC_CONV - 1 - s:SC_CONV - s, :] * extc[SUBLANES - s:SUBLANES - s + lt, :]
    extc[0:SUBLANES, :] = ch[lt - SUBLANES:lt, :]
    o_ref[:, GDN_V:GDN_V + scw] = (b_gate * y).astype(o_ref.dtype)

    @pl.when(t == nt - 1)
    def _():
        st_ref[...] = s_scr[...]
        qb_ref[...] = u_pre[lt - SUBLANES:lt, :]
        chb_ref[...] = ch[lt - SUBLANES:lt, :]


def _gdn_prompt(proj, conv_w, sconv_w, alog_v, dtb_v, g_norm, layer, *, batch, seq, n_rows_out, d_model):
    lt = _pick_tile(seq, 256)
    assert lt % CHUNK == 0
    nt = seq // lt
    cols = proj.shape[1]
    scw = sconv_w.shape[2]
    kern = functools.partial(_gdn_prompt_kernel, lt=lt)
    return pl.pallas_call(
        kern,
        grid=(batch, nt),
        in_specs=[
            pl.BlockSpec((lt, cols), lambda b, t: (b * nt + t, 0)),
            pl.BlockSpec((None, GDN_CONV, QKV_WIDTH), lambda b, t: (layer, 0, 0)),
            pl.BlockSpec((None, SC_CONV, scw), lambda b, t: (layer, 0, 0)),
            pl.BlockSpec((None, 1, LANES), lambda b, t: (layer, 0, 0)),
            pl.BlockSpec((None, 1, LANES), lambda b, t: (layer, 0, 0)),
            pl.BlockSpec((None, 1, GDN_DV), lambda b, t: (layer, 0, 0)),
        ],
        out_specs=[
            pl.BlockSpec((lt, d_model), lambda b, t: (b * nt + t, 0)),
            pl.BlockSpec((None, GDN_HEADS, GDN_DK, GDN_DV), lambda b, t: (b, 0, 0, 0)),
            pl.BlockSpec((None, SUBLANES, QKV_WIDTH), lambda b, t: (b, 0, 0)),
            pl.BlockSpec((None, SUBLANES, scw), lambda b, t: (b, 0, 0)),
        ],
        out_shape=[
            jax.ShapeDtypeStruct((n_rows_out, d_model), BF16),
            jax.ShapeDtypeStruct((batch, GDN_HEADS, GDN_DK, GDN_DV), F32),
            jax.ShapeDtypeStruct((batch, SUBLANES, QKV_WIDTH), F32),
            jax.ShapeDtypeStruct((batch, SUBLANES, scw), F32),
        ],
        scratch_shapes=[
            pltpu.VMEM((GDN_HEADS, GDN_DK, GDN_DV), F32),
            pltpu.VMEM((lt + SUBLANES, QKV_WIDTH), F32),
            pltpu.VMEM((lt + SUBLANES, scw), F32),
        ],
        compiler_params=_params(("arbitrary", "arbitrary")),
        name="gdn_prompt",
    )(proj, conv_w, sconv_w, alog_v, dtb_v, g_norm)


def _shift_in_segments(x, buf, s, buf_rows, seg):
    n = x.shape[0]
    rin = _iota2(x.shape, 0) & (seg - 1)
    from_x = pltpu.roll(x, s, axis=0)
    back = (n + s - buf_rows) % n
    from_buf = pltpu.roll(buf, back, axis=0) if back else buf
    return jnp.where(rin >= s, from_x, from_buf)


def _gdn_sample_kernel(mix_in_ref, proj_ref, st_in_ref, qbuf_ref, cbuf_ref, cw_ref, scw_ref,
                       alog_ref, dtb_ref, gn_ref,
                       o_ref, st_ref, ch_ref, *, seq):
    del mix_in_ref
    n = proj_ref.shape[0]
    nseq = n // seq
    lseq = seq.bit_length() - 1
    heads = range(GDN_HEADS)
    u_pre = proj_ref[:, 0:QKV_WIDTH]
    qbuf = qbuf_ref[...]
    conv = cw_ref[GDN_CONV - 1:GDN_CONV, :] * u_pre
    for s in range(1, GDN_CONV):
        conv = conv + cw_ref[GDN_CONV - 1 - s:GDN_CONV - s, :] * _shift_in_segments(
            u_pre, qbuf, s, GDN_CONV - 1, seq)
    qkv = _silu(conv)

    u, w, qk, qd, kt, eglast, tiles = _heads_prep(
        qkv, proj_ref[:, COL_SCB + 3 * GDN_V:COL_SCB + 3 * GDN_V + LANES],
        alog_ref[...], dtb_ref[...], seq)

    rseq = _iota2((n, GDN_DV), 0) >> lseq
    kt_t = [kt[h].T for h in heads]
    v_parts = [[] for _ in heads]
    o_parts = [[] for _ in heads]
    for s in range(nseq):
        rs = slice(s * seq, (s + 1) * seq)
        ws = [_dot(jnp.concatenate([w[h][rs], qd[h][rs]], axis=0), st_in_ref[s, h]) for h in heads]
        for h in heads:
            v_parts[h].append(u[h][rs] - ws[h][:seq])
            o_parts[h].append(ws[h][seq:])
    v_new = [jnp.concatenate(v_parts[h], axis=0) for h in heads]
    z = proj_ref[:, COL_Z:COL_Z + GDN_V]
    for h in heads:
        hs = slice(h * GDN_DV, (h + 1) * GDN_DV)
        o = jnp.concatenate(o_parts[h], axis=0) + jnp.concatenate(
            [_dot(qk[h][j], v_new[h][rt]) for j, rt in enumerate(tiles)], axis=0)
        o_ref[:, hs] = _gated_out_norm(o, z[:, hs], gn_ref[...]).astype(o_ref.dtype)
    for s in range(nseq):
        for h in heads:
            v_s = jnp.where(rseq == s, v_new[h], 0.0)
            st_ref[s, h] = (st_in_ref[s, h]
                            * eglast[s * seq:s * seq + 1, GATE_A_LANE + h:GATE_A_LANE + h + 1]
                            + _dot(kt_t[h], v_s))

    scw = cbuf_ref.shape[1]
    b_gate = proj_ref[:, COL_SCB:COL_SCB + scw]
    ch = proj_ref[:, COL_SCB + scw:COL_SCB + 2 * scw] * proj_ref[:, COL_SCB + 2 * scw:COL_SCB + 3 * scw]
    cbuf = cbuf_ref[...]
    y = scw_ref[SC_CONV - 1:SC_CONV, :] * ch
    for s in range(1, SC_CONV):
        y = y + scw_ref[SC_CONV - 1 - s:SC_CONV - s, :] * _shift_in_segments(ch, cbuf, s, SC_CONV - 1, seq)
    o_ref[:, GDN_V:GDN_V + scw] = (b_gate * y).astype(o_ref.dtype)
    ch_ref[...] = ch


def _gdn_sample(mix, proj, state, qbuf, cbuf, conv_w, sconv_w, alog_v, dtb_v, g_norm, layer, *,
                row0, n_seq, seq, d_model):
    assert seq == SUBLANES
    rows = _pick_tile(n_seq * seq, 128)
    spb = rows // seq
    assert n_seq % spb == 0 and row0 % rows == 0
    blk0 = row0 // rows
    cols = proj.shape[1]
    scw = sconv_w.shape[2]
    kern = functools.partial(_gdn_sample_kernel, seq=seq)
    return pl.pallas_call(
        kern,
        grid=(n_seq // spb,),
        in_specs=[
            pl.BlockSpec(memory_space=pl.ANY),
            pl.BlockSpec((rows, cols), lambda i: (blk0 + i, 0)),
            pl.BlockSpec((None, spb, GDN_HEADS, GDN_DK, GDN_DV), lambda i: (layer, i, 0, 0, 0)),
            pl.BlockSpec((rows, QKV_WIDTH), lambda i: (i, 0)),
            pl.BlockSpec((rows, scw), lambda i: (i, 0)),
            pl.BlockSpec((None, GDN_CONV, QKV_WIDTH), lambda i: (layer, 0, 0)),
            pl.BlockSpec((None, SC_CONV, scw), lambda i: (layer, 0, 0)),
            pl.BlockSpec((None, 1, LANES), lambda i: (layer, 0, 0)),
            pl.BlockSpec((None, 1, LANES), lambda i: (layer, 0, 0)),
            pl.BlockSpec((None, 1, GDN_DV), lambda i: (layer, 0, 0)),
        ],
        out_specs=[
            pl.BlockSpec((rows, d_model), lambda i: (blk0 + i, 0)),
            pl.BlockSpec((spb, GDN_HEADS, GDN_DK, GDN_DV), lambda i: (i, 0, 0, 0)),
            pl.BlockSpec((rows, scw), lambda i: (i, 0)),
        ],
        out_shape=[
            jax.ShapeDtypeStruct(mix.shape, mix.dtype),
            jax.ShapeDtypeStruct((n_seq, GDN_HEADS, GDN_DK, GDN_DV), F32),
            jax.ShapeDtypeStruct((n_seq * seq, scw), F32),
        ],
        input_output_aliases={0: 0},
        compiler_params=_params(("arbitrary",)),
        name="gdn_sample",
    )(mix, proj, state, qbuf, cbuf, conv_w, sconv_w, alog_v, dtb_v, g_norm)


def _softmax_rows(s):
    m = jnp.max(s, axis=-1, keepdims=True)
    p = jnp.exp(s - m)
    return p / jnp.sum(p, axis=-1, keepdims=True)


def _attn_prompt_kernel(q_ref, k_ref, v_ref, o_ref):
    hd = q_ref.shape[1] // X_HEADS
    for h in range(X_HEADS):
        hs = slice(h * hd, (h + 1) * hd)
        p = _softmax_rows(_dot_nt(q_ref[:, hs], k_ref[:, hs]) * (hd ** -0.5))
        o_ref[:, hs] = _dot(p, v_ref[:, hs]).astype(o_ref.dtype)


def _attn_prompt(q, mk, mv, *, batch, seq, mem_len):
    d = q.shape[1]
    tq = _pick_tile(seq, 512)
    nq = seq // tq
    return pl.pallas_call(
        _attn_prompt_kernel,
        grid=(batch, nq),
        in_specs=[
            pl.BlockSpec((tq, d), lambda b, t: (b * nq + t, 0)),
            pl.BlockSpec((mem_len, d), lambda b, t: (b, 0)),
            pl.BlockSpec((mem_len, d), lambda b, t: (b, 0)),
        ],
        out_specs=pl.BlockSpec((tq, d), lambda b, t: (b * nq + t, 0)),
        out_shape=jax.ShapeDtypeStruct(q.shape, BF16),
        compiler_params=_params(("arbitrary", "arbitrary")),
        name="attn_prompt",
    )(q, mk, mv)


def _attn_sample_kernel(att_in_ref, q_ref, k_ref, v_ref, o_ref, *, seq, spb):
    del att_in_ref
    mem_len, nh, hd = k_ref.shape[1:]
    lseq = seq.bit_length() - 1
    rows_h = _iota2((nh * seq, mem_len * nh), 0) >> lseq
    cols_h = _iota2((nh * seq, mem_len * nh), 1) & (nh - 1)
    own_head = rows_h == cols_h
    for s in range(spb):
        rs = slice(s * seq, (s + 1) * seq)
        q = q_ref[rs, :]
        q_all = jnp.concatenate([q[:, h * hd:(h + 1) * hd] for h in range(nh)], axis=0)
        k_all = k_ref[s].reshape(mem_len * nh, hd)
        v_all = v_ref[s].reshape(mem_len * nh, hd)
        sc = jnp.where(own_head, _dot_nt(q_all, k_all) * (hd ** -0.5), -jnp.inf)
        o_all = _dot(_softmax_rows(sc), v_all)
        for h in range(nh):
            o_ref[rs, h * hd:(h + 1) * hd] = o_all[h * seq:(h + 1) * seq].astype(o_ref.dtype)


def _attn_sample(att, q, cache_k, cache_v, layer, *, row0, n_seq, seq):
    d = q.shape[1]
    _, _, mem_len, nh, hd = cache_k.shape
    assert nh & (nh - 1) == 0
    spb = 4
    assert n_seq % spb == 0 and row0 % (spb * seq) == 0
    rows = spb * seq
    blk0 = row0 // rows
    kern = functools.partial(_attn_sample_kernel, seq=seq, spb=spb)
    return pl.pallas_call(
        kern,
        grid=(n_seq // spb,),
        in_specs=[
            pl.BlockSpec(memory_space=pl.ANY),
            pl.BlockSpec((rows, d), lambda i: (blk0 + i, 0)),
            pl.BlockSpec((None, spb, mem_len, nh, hd), lambda i: (layer, i, 0, 0, 0)),
            pl.BlockSpec((None, spb, mem_len, nh, hd), lambda i: (layer, i, 0, 0, 0)),
        ],
        out_specs=pl.BlockSpec((rows, d), lambda i: (blk0 + i, 0)),
        out_shape=jax.ShapeDtypeStruct(att.shape, att.dtype),
        input_output_aliases={0: 0},
        compiler_params=_params(("arbitrary",)),
        name="attn_sample",
    )(att, q, cache_k, cache_v)


def _gate_lane_vector(v, lane0):
    depth, n = v.shape
    out = jnp.zeros((depth, 1, LANES), F32)
    return out.at[:, 0, lane0:lane0 + n].set(v.astype(F32))


def kernel(x_prompt, x_sample, mem_prompt, state_gdn, state_qkv_conv, state_short_conv, cache_mem_k, cache_mem_v, g_ffn1, w_ffn1_gu, w_ffn1_down, g_mix, w_in, conv_qkv_w, a_log, dt_bias, g_gdn_out, sconv_w, w_out, g_xattn, w_xq, w_xk, w_xv, w_xo, g_ffn2, w_ffn2_gu, w_ffn2_down, g_final):
    batch, seq, d = x_prompt.shape
    dec_batch, dec_seq, _ = x_sample.shape
    depth = w_in.shape[0]
    mem_len = mem_prompt.shape[1]
    scw = sconv_w.shape[2]
    n_p = batch * seq
    n_s = dec_batch * dec_seq
    n_all = n_p + n_s
    off_beta = QKV_WIDTH + GDN_V
    off_sc = off_beta + 2 * GDN_HEADS

    w_in_p = jnp.concatenate(
        [w_in[:, :, :off_beta], w_in[:, :, off_sc:], w_in[:, :, off_beta:off_sc],
         jnp.zeros((depth, d, LANES - 2 * GDN_HEADS), w_in.dtype)], axis=2).astype(BF16)
    wgu1, wd1 = w_ffn1_gu.astype(BF16), w_ffn1_down.astype(BF16)
    wgu2, wd2 = w_ffn2_gu.astype(BF16), w_ffn2_down.astype(BF16)
    w_out_b, w_xq_b, w_xo_b = w_out.astype(BF16), w_xq.astype(BF16), w_xo.astype(BF16)
    w_xk_b, w_xv_b = w_xk.astype(BF16), w_xv.astype(BF16)
    alog_v = _gate_lane_vector(a_log, GATE_A_LANE)
    dtb_v = _gate_lane_vector(dt_bias, GATE_A_LANE)
    g1, gm, gx, g2 = (t.reshape(depth, 1, d) for t in (g_ffn1, g_mix, g_xattn, g_ffn2))
    gn = g_gdn_out.reshape(depth, 1, GDN_DV)
    gf = g_final.reshape(1, d)
    mem2 = mem_prompt.reshape(batch * mem_len, d)
    qbuf_all = jnp.pad(state_qkv_conv, ((0, 0), (0, 0), (0, dec_seq - (GDN_CONV - 1)), (0, 0)))
    cbuf_all = jnp.pad(state_short_conv, ((0, 0), (0, 0), (0, dec_seq - (SC_CONV - 1)), (0, 0)))

    x = jnp.concatenate([x_prompt.reshape(n_p, d), x_sample.reshape(n_s, d)], axis=0)

    p_s, p_qb, p_sb, p_mk, p_mv, s_s, s_qb, s_sb = [], [], [], [], [], [], [], []
    for l in range(depth):
        x = _ffn(x, g1, wgu1, wd1, gf, l, final_norm=False)
        proj = _matmul(x, w_in_p, l, g=gm, name="in_proj")

        mix, st_p, qb_p, chb_p = _gdn_prompt(proj, conv_qkv_w, sconv_w, alog_v, dtb_v, gn, l,
                                             batch=batch, seq=seq, n_rows_out=n_all, d_model=d)
        mix, st_s, ch_s = _gdn_sample(mix, proj, state_gdn, qbuf_all[l].reshape(n_s, QKV_WIDTH),
                                      cbuf_all[l].reshape(n_s, scw), conv_qkv_w, sconv_w,
                                      alog_v, dtb_v, gn, l,
                                      row0=n_p, n_seq=dec_batch, seq=dec_seq, d_model=d)
        x = _matmul(mix, w_out_b, l, res=x, name="out_proj")

        mk = _matmul(mem2, w_xk_b, l, name="mem_k")
        mv = _matmul(mem2, w_xv_b, l, name="mem_v")
        q = _matmul(x, w_xq_b, l, g=gx, out_dtype=BF16, name="xq_proj")
        att = _attn_prompt(q, mk, mv, batch=batch, seq=seq, mem_len=mem_len)
        att = _attn_sample(att, q, cache_mem_k, cache_mem_v, l, row0=n_p, n_seq=dec_batch, seq=dec_seq)
        x = _matmul(att, w_xo_b, l, res=x, name="xo_proj")

        x = _ffn(x, g2, wgu2, wd2, gf, l, final_norm=(l == depth - 1))

        p_s.append(st_p)
        p_qb.append(qb_p[:, SUBLANES - (GDN_CONV - 1):, :])
        p_sb.append(chb_p[:, SUBLANES - (SC_CONV - 1):, :])
        p_mk.append(mk.reshape(batch, mem_len, X_HEADS, d // X_HEADS))
        p_mv.append(mv.reshape(batch, mem_len, X_HEADS, d // X_HEADS))
        s_s.append(st_s)
        s_qb.append(proj.reshape(n_all // dec_seq, dec_seq, -1)[n_p // dec_seq:, dec_seq - (GDN_CONV - 1):, :QKV_WIDTH])
        s_sb.append(ch_s.reshape(dec_batch, dec_seq, scw)[:, dec_seq - (SC_CONV - 1):, :])

    y_prompt = x[:n_p].reshape(batch, seq, d)
    y_sample = x[n_p:].reshape(dec_batch, dec_seq, d)
    return (y_prompt, y_sample,
            jnp.stack(p_s), jnp.stack(p_qb), jnp.stack(p_sb), jnp.stack(p_mk), jnp.stack(p_mv),
            jnp.stack(s_s), jnp.stack(s_qb), jnp.stack(s_sb))
```

```python
import functools

import jax
import jax.numpy as jnp
from jax import lax
from jax.experimental import pallas as pl
from jax.experimental.pallas import tpu as pltpu

F32 = jnp.float32
BF16 = jnp.bfloat16

GDN_HEADS = 4
GDN_DK = 128
GDN_DV = 128
GDN_QK = GDN_HEADS * GDN_DK
GDN_V = GDN_HEADS * GDN_DV
QKV_WIDTH = 2 * GDN_QK + GDN_V
GDN_CONV = 4
SC_CONV = 3
CHUNK = 64
MAT_TILE = 128
X_HEADS = 4
RMS_EPS = 1e-6
LANES = 128
SUBLANES = 8
VMEM_LIMIT = 56 * 1024 * 1024
ROW_TILE = 512

COL_Z = QKV_WIDTH
COL_SCB = COL_Z + GDN_V
GATE_BETA_LANE = 0
GATE_A_LANE = GDN_HEADS


def _params(sem):
    return pltpu.CompilerParams(dimension_semantics=sem, vmem_limit_bytes=VMEM_LIMIT)


def _pick_tile(n, pref):
    t = min(pref, n)
    while n % t:
        t -= SUBLANES
    assert t > 0 and t % SUBLANES == 0
    return t


def _silu(x):
    return x * jax.nn.sigmoid(x)


def _rms(x, g):
    return x * lax.rsqrt(jnp.mean(x * x, axis=-1, keepdims=True) + RMS_EPS) * g


def _dot(a, b):
    return jnp.dot(a.astype(BF16), b.astype(BF16), preferred_element_type=F32)


def _dot_nt(a, b):
    return lax.dot_general(a.astype(BF16), b.astype(BF16), (((1,), (1,)), ((), ())),
                           preferred_element_type=F32)


def _iota2(shape, dim):
    return lax.broadcasted_iota(jnp.int32, shape, dim)


def _ffn_kernel(*refs, d_ff, col_chunk, final_norm, n_in, n_out, split_tile):
    x_refs = refs[:n_in]
    g_ref, wgu_ref, wd_ref, gf_ref = refs[n_in:n_in + 4]
    o_refs = refs[n_in + 4:n_in + 4 + n_out]
    h_ref = refs[n_in + 4 + n_out]
    i = pl.program_id(0)
    if n_in == 2:
        x = jnp.where(i < split_tile, x_refs[0][...], x_refs[1][...])
    else:
        x = x_refs[0][...]
    xb = _rms(x, g_ref[...]).astype(BF16)
    for c in range(d_ff // col_chunk):
        lo = c * col_chunk
        gate = jnp.dot(xb, wgu_ref[:, lo:lo + col_chunk], preferred_element_type=F32)
        up = jnp.dot(xb, wgu_ref[:, d_ff + lo:d_ff + lo + col_chunk], preferred_element_type=F32)
        h_ref[:, lo:lo + col_chunk] = (_silu(gate) * up).astype(BF16)
    y = x + 0.5 * jnp.dot(h_ref[...], wd_ref[...], preferred_element_type=F32)
    if final_norm:
        y = _rms(y, gf_ref[...])
    if n_out == 2:
        @pl.when(i < split_tile)
        def _():
            o_refs[0][...] = y

        @pl.when(i >= split_tile)
        def _():
            o_refs[1][...] = y
    else:
        o_refs[0][...] = y


def _ffn(xs, g, wgu, wd, g_final, layer, *, final_norm, split_rows=None):
    d = xs[0].shape[1]
    n = sum(a.shape[0] for a in xs)
    d_ff = wd.shape[1]
    tm = ROW_TILE
    col_chunk = 256
    assert d_ff % col_chunk == 0 and all(a.shape[0] % tm == 0 for a in xs)
    if len(xs) == 2:
        split_tile = xs[0].shape[0] // tm
        x_specs = [pl.BlockSpec((tm, d), lambda i: (jnp.minimum(i, split_tile - 1), 0)),
                   pl.BlockSpec((tm, d), lambda i: (jnp.maximum(i - split_tile, 0), 0))]
    else:
        split_tile = None
        x_specs = [pl.BlockSpec((tm, d), lambda i: (i, 0))]
    if split_rows is not None:
        assert split_rows % tm == 0
        split_tile = split_rows // tm
        out_specs = [pl.BlockSpec((tm, d), lambda i: (jnp.minimum(i, split_tile - 1), 0)),
                     pl.BlockSpec((tm, d), lambda i: (jnp.maximum(i - split_tile, 0), 0))]
        out_shape = [jax.ShapeDtypeStruct((split_rows, d), F32),
                     jax.ShapeDtypeStruct((n - split_rows, d), F32)]
    else:
        out_specs = [pl.BlockSpec((tm, d), lambda i: (i, 0))]
        out_shape = [jax.ShapeDtypeStruct((n, d), F32)]
    kern = functools.partial(_ffn_kernel, d_ff=d_ff, col_chunk=col_chunk, final_norm=final_norm,
                             n_in=len(xs), n_out=len(out_specs), split_tile=split_tile)
    out = pl.pallas_call(
        kern,
        grid=(n // tm,),
        in_specs=x_specs + [
            pl.BlockSpec((None, 1, d), lambda i: (layer, 0, 0)),
            pl.BlockSpec((None, d, 2 * d_ff), lambda i: (layer, 0, 0)),
            pl.BlockSpec((None, d_ff, d), lambda i: (layer, 0, 0)),
            pl.BlockSpec((1, d), lambda i: (0, 0)),
        ],
        out_specs=out_specs,
        out_shape=out_shape,
        scratch_shapes=[pltpu.VMEM((tm, d_ff), BF16)],
        compiler_params=_params(("arbitrary",)),
        name="ffn",
    )(*xs, g, wgu, wd, g_final)
    return out if split_rows is not None else out[0]


def _mm_kernel(*refs, rms, residual, aliased):
    it = iter(refs)
    if aliased:
        next(it)
    x_ref = next(it)
    g_ref = next(it) if rms else None
    w_ref = next(it)
    r_ref = next(it) if residual else None
    o_ref = next(it)
    x = x_ref[...]
    if rms:
        x = _rms(x, g_ref[...])
    acc = jnp.dot(x.astype(BF16), w_ref[...], preferred_element_type=F32)
    if residual:
        acc = acc + r_ref[...]
    o_ref[...] = acc.astype(o_ref.dtype)


def _matmul(x, w, layer, *, n_rows, x_row0=0, g=None, res=None, res_row0=0, into=None, out_row0=0,
            out_dtype=F32, name="mm"):
    k = x.shape[1]
    n_out = w.shape[2]
    tm = _pick_tile(n_rows, ROW_TILE)
    assert x_row0 % tm == 0 and res_row0 % tm == 0 and out_row0 % tm == 0
    xb0, rb0, ob0 = x_row0 // tm, res_row0 // tm, out_row0 // tm
    args, specs = [], []
    if into is not None:
        args.append(into)
        specs.append(pl.BlockSpec(memory_space=pl.ANY))
    args.append(x)
    specs.append(pl.BlockSpec((tm, k), lambda i: (xb0 + i, 0)))
    if g is not None:
        args.append(g)
        specs.append(pl.BlockSpec((None, 1, k), lambda i: (layer, 0, 0)))
    args.append(w)
    specs.append(pl.BlockSpec((None, k, n_out), lambda i: (layer, 0, 0)))
    if res is not None:
        args.append(res)
        specs.append(pl.BlockSpec((tm, n_out), lambda i: (rb0 + i, 0)))
    kern = functools.partial(_mm_kernel, rms=g is not None, residual=res is not None,
                             aliased=into is not None)
    out_shape = (jax.ShapeDtypeStruct(into.shape, into.dtype) if into is not None
                 else jax.ShapeDtypeStruct((n_rows, n_out), out_dtype))
    return pl.pallas_call(
        kern,
        grid=(n_rows // tm,),
        in_specs=specs,
        out_specs=pl.BlockSpec((tm, n_out), lambda i: (ob0 + i, 0)),
        out_shape=out_shape,
        input_output_aliases={0: 0} if into is not None else {},
        compiler_params=_params(("arbitrary",)),
        name=name,
    )(*args)


def _unit_lower_inverse(a_list, merge_levels):
    n = a_list[0].shape[0]
    r = _iota2((n, n), 0)
    c = _iota2((n, n), 1)
    eye = (r == c).astype(F32)
    blk8 = (r >> 3) == (c >> 3)
    d = [jnp.where(blk8, a, 0.0) for a in a_list]
    d2 = [_dot(x, x) for x in d]
    p = [_dot(eye - x, eye + y) for x, y in zip(d, d2)]
    d4 = [_dot(y, y) for y in d2]
    t = [_dot(x, eye + y) for x, y in zip(p, d4)]
    for lvl in merge_levels:
        mask = ((r >> (lvl + 1)) == (c >> (lvl + 1))) & ((r >> lvl) != (c >> lvl))
        lt = [_dot(jnp.where(mask, a, 0.0), x) for a, x in zip(a_list, t)]
        t = [x - _dot(x, y) for x, y in zip(t, lt)]
    return t


def _l2n(t):
    return t * lax.rsqrt(jnp.sum(t * t, axis=-1, keepdims=True) + 1e-6)


def _segment_cumsum(g, seg):
    rin = _iota2(g.shape, 0) & (seg - 1)
    sh = 1
    while sh < seg:
        g = g + jnp.where(rin >= sh, pltpu.roll(g, sh, axis=0), 0.0)
        sh *= 2
    return g


def _segment_last(gc, seg):
    n = gc.shape[0]
    rin = _iota2(gc.shape, 0) & (seg - 1)
    x = jnp.where(rin == seg - 1, gc, 0.0)
    sh = 1
    while sh < seg:
        x = x + jnp.where(rin + sh <= seg - 1, pltpu.roll(x, n - sh, axis=0), 0.0)
        sh *= 2
    return x


def _gates(gate_blk, alog, dtb, seg):
    beta = jax.nn.sigmoid(gate_blk)
    xg = gate_blk + dtb
    softplus = jnp.maximum(xg, 0.0) + jnp.log1p(jnp.exp(-jnp.abs(xg)))
    g = -jnp.exp(alog) * softplus
    gc = _segment_cumsum(g, seg)
    return beta, gc


def _heads_prep(qkv, gate_blk, alog, dtb, seg):
    n = qkv.shape[0]
    mt = min(n, MAT_TILE)
    assert n % mt == 0 and mt % seg == 0
    tiles = [slice(j * mt, (j + 1) * mt) for j in range(n // mt)]
    lg = seg.bit_length() - 1
    beta, gc = _gates(gate_blk, alog, dtb, seg)
    egc = jnp.exp(gc)
    glast = _segment_last(gc, seg)
    ekt = jnp.exp(glast - gc)
    eglast = jnp.exp(glast)
    gc_t = gc.T
    r = _iota2((mt, mt), 0)
    c = _iota2((mt, mt), 1)
    same = (r >> lg) == (c >> lg)
    tri = (r >= c) & same
    strict = (r > c) & same
    heads = range(GDN_HEADS)
    q = [_l2n(qkv[:, h * GDN_DK:(h + 1) * GDN_DK]) * (GDN_DK ** -0.5) for h in heads]
    k = [_l2n(qkv[:, GDN_QK + h * GDN_DK:GDN_QK + (h + 1) * GDN_DK]) for h in heads]
    v = [qkv[:, 2 * GDN_QK + h * GDN_DV:2 * GDN_QK + (h + 1) * GDN_DV] for h in heads]
    bh = [beta[:, GATE_BETA_LANE + h:GATE_BETA_LANE + h + 1] for h in heads]
    eg = [egc[:, GATE_A_LANE + h:GATE_A_LANE + h + 1] for h in heads]
    kb = [k[h] * bh[h] for h in heads]
    rhs = [jnp.concatenate([v[h] * bh[h], kb[h] * eg[h]], axis=1) for h in heads]
    dec = [[None] * len(tiles) for _ in heads]
    for h in heads:
        al = GATE_A_LANE + h
        for j, rt in enumerate(tiles):
            diff = gc[rt, al:al + 1] - gc_t[al:al + 1, rt]
            dec[h][j] = jnp.where(tri, jnp.exp(jnp.where(tri, diff, 0.0)), 0.0)
    systems = [(h, j) for h in heads for j in range(len(tiles))]
    a = [jnp.where(strict, _dot_nt(kb[h][tiles[j]], k[h][tiles[j]]) * dec[h][j], 0.0)
         for h, j in systems]
    tinv = _unit_lower_inverse(a, merge_levels=tuple(range(3, lg)))
    uw = [_dot(t, rhs[h][tiles[j]]) for t, (h, j) in zip(tinv, systems)]
    uw = [jnp.concatenate(uw[h * len(tiles):(h + 1) * len(tiles)], axis=0) for h in heads]
    u = [x[:, :GDN_DV] for x in uw]
    w = [x[:, GDN_DV:] for x in uw]
    qk = [[_dot_nt(q[h][rt], k[h][rt]) * dec[h][j] for j, rt in enumerate(tiles)] for h in heads]
    qd = [q[h] * eg[h] for h in heads]
    kt = [k[h] * ekt[:, GATE_A_LANE + h:GATE_A_LANE + h + 1] for h in heads]
    return u, w, qk, qd, kt, eglast, tiles


def _gated_out_norm(o, z, gn):
    return _rms(o, gn) * _silu(z)


def _in_proj_cols(x, g_ref, win_ref):
    xb = _rms(x, g_ref[...]).astype(BF16)

    def cols(lo, width):
        return jnp.dot(xb, win_ref[:, lo:lo + width], preferred_element_type=F32)

    return cols


def _in_proj(x, g_ref, win_ref, scw):
    cols = _in_proj_cols(x, g_ref, win_ref)
    return (cols(0, QKV_WIDTH), cols(COL_Z, GDN_V), cols(COL_SCB, scw), cols(COL_SCB + scw, scw),
            cols(COL_SCB + 2 * scw, scw), cols(COL_SCB + 3 * scw, LANES))


def _mixer_prompt_kernel(x_ref, g_ref, win_ref, cw_ref, scw_ref, alog_ref, dtb_ref, gn_ref, wout_ref,
                         o_ref, st_ref, qb_ref, chb_ref,
                         s_scr, extq, extc, mix_scr, *, lt):
    t = pl.program_id(1)
    nt = pl.num_programs(1)
    heads = range(GDN_HEADS)
    scw = extc.shape[1]

    @pl.when(t == 0)
    def _():
        s_scr[...] = jnp.zeros_like(s_scr)
        extq[0:SUBLANES, :] = jnp.zeros((SUBLANES, QKV_WIDTH), F32)
        extc[0:SUBLANES, :] = jnp.zeros((SUBLANES, scw), F32)

    x = x_ref[...]
    cols = _in_proj_cols(x, g_ref, win_ref)
    u_pre = cols(0, QKV_WIDTH)
    gate_blk = cols(COL_SCB + 3 * scw, LANES)
    pending = [(COL_Z, GDN_V), (COL_SCB, scw), (COL_SCB + scw, scw), (COL_SCB + 2 * scw, scw)]
    late = []

    extq[SUBLANES:SUBLANES + lt, :] = u_pre
    conv = cw_ref[GDN_CONV - 1:GDN_CONV, :] * u_pre
    for s in range(1, GDN_CONV):
        conv = conv + cw_ref[GDN_CONV - 1 - s:GDN_CONV - s, :] * extq[SUBLANES - s:SUBLANES - s + lt, :]
    extq[0:SUBLANES, :] = u_pre[lt - SUBLANES:lt, :]
    qkv = _silu(conv)

    u, w, qk, qd, kt, eglast, tiles = _heads_prep(qkv, gate_blk, alog_ref[...], dtb_ref[...], CHUNK)

    s_cur = [s_scr[h] for h in heads]
    v_parts = [[] for _ in heads]
    o_parts = [[] for _ in heads]
    for c in range(lt // CHUNK):
        row0 = c * CHUNK
        rs = slice(row0, row0 + CHUNK)
        ws = [_dot(jnp.concatenate([w[h][rs], qd[h][rs]], axis=0), s_cur[h]) for h in heads]
        vn = [u[h][rs] - ws[h][:CHUNK] for h in heads]
        s_cur = [s_cur[h] * eglast[row0:row0 + 1, GATE_A_LANE + h:GATE_A_LANE + h + 1]
                 + _dot(kt[h][rs].T, vn[h]) for h in heads]
        for h in heads:
            v_parts[h].append(vn[h])
            o_parts[h].append(ws[h][CHUNK:])
        if pending:
            late.append(cols(*pending.pop(0)))
    while pending:
        late.append(cols(*pending.pop(0)))
    z, b_gate, c_gate, h_gate = late
    for h in heads:
        s_scr[h] = s_cur[h]
        v_new = jnp.concatenate(v_parts[h], axis=0)
        o = jnp.concatenate(o_parts[h], axis=0) + jnp.concatenate(
            [_dot(qk[h][j], v_new[rt]) for j, rt in enumerate(tiles)], axis=0)
        hs = slice(h * GDN_DV, (h + 1) * GDN_DV)
        mix_scr[:, hs] = _gated_out_norm(o, z[:, hs], gn_ref[...]).astype(BF16)

    ch = c_gate * h_gate
    extc[SUBLANES:SUBLANES + lt, :] = ch
    y = scw_ref[SC_CONV - 1:SC_CONV, :] * ch
    for s in range(1, SC_CONV):
        y = y + scw_ref[SC_CONV - 1 - s:SC_CONV - s, :] * extc[SUBLANES - s:SUBLANES - s + lt, :]
    extc[0:SUBLANES, :] = ch[lt - SUBLANES:lt, :]
    mix_scr[:, GDN_V:GDN_V + scw] = (b_gate * y).astype(BF16)

    o_ref[...] = x + jnp.dot(mix_scr[...], wout_ref[...], preferred_element_type=F32)

    @pl.when(t == nt - 1)
    def _():
        st_ref[...] = s_scr[...]
        qb_ref[...] = u_pre[lt - SUBLANES:lt, :]
        chb_ref[...] = ch[lt - SUBLANES:lt, :]


def _mixer_prompt(x, g, w_in, conv_w, sconv_w, alog_v, dtb_v, g_norm, w_out, layer, *, batch, seq):
    n_all, d = x.shape
    lt = _pick_tile(seq, 256)
    assert lt % CHUNK == 0
    nt = seq // lt
    cols = w_in.shape[2]
    scw = sconv_w.shape[2]
    kern = functools.partial(_mixer_prompt_kernel, lt=lt)
    return pl.pallas_call(
        kern,
        grid=(batch, nt),
        in_specs=[
            pl.BlockSpec((lt, d), lambda b, t: (b * nt + t, 0)),
            pl.BlockSpec((None, 1, d), lambda b, t: (layer, 0, 0)),
            pl.BlockSpec((None, d, cols), lambda b, t: (layer, 0, 0)),
            pl.BlockSpec((None, GDN_CONV, QKV_WIDTH), lambda b, t: (layer, 0, 0)),
            pl.BlockSpec((None, SC_CONV, scw), lambda b, t: (layer, 0, 0)),
            pl.BlockSpec((None, 1, LANES), lambda b, t: (layer, 0, 0)),
            pl.BlockSpec((None, 1, LANES), lambda b, t: (layer, 0, 0)),
            pl.BlockSpec((None, 1, GDN_DV), lambda b, t: (layer, 0, 0)),
            pl.BlockSpec((None, GDN_V + scw, d), lambda b, t: (layer, 0, 0)),
        ],
        out_specs=[
            pl.BlockSpec((lt, d), lambda b, t: (b * nt + t, 0)),
            pl.BlockSpec((None, GDN_HEADS, GDN_DK, GDN_DV), lambda b, t: (b, 0, 0, 0)),
            pl.BlockSpec((None, SUBLANES, QKV_WIDTH), lambda b, t: (b, 0, 0)),
            pl.BlockSpec((None, SUBLANES, scw), lambda b, t: (b, 0, 0)),
        ],
        out_shape=[
            jax.ShapeDtypeStruct((n_all, d), F32),
            jax.ShapeDtypeStruct((batch, GDN_HEADS, GDN_DK, GDN_DV), F32),
            jax.ShapeDtypeStruct((batch, SUBLANES, QKV_WIDTH), F32),
            jax.ShapeDtypeStruct((batch, SUBLANES, scw), F32),
        ],
        scratch_shapes=[
            pltpu.VMEM((GDN_HEADS, GDN_DK, GDN_DV), F32),
            pltpu.VMEM((lt + SUBLANES, QKV_WIDTH), F32),
            pltpu.VMEM((lt + SUBLANES, scw), F32),
            pltpu.VMEM((lt, GDN_V + scw), BF16),
        ],
        compiler_params=_params(("arbitrary", "arbitrary")),
        name="mixer_prompt",
    )(x, g, w_in, conv_w, sconv_w, alog_v, dtb_v, g_norm, w_out)


def _shift_in_segments(x, buf, s, buf_rows, seg):
    n = x.shape[0]
    rin = _iota2(x.shape, 0) & (seg - 1)
    from_x = pltpu.roll(x, s, axis=0)
    back = (n + s - buf_rows) % n
    from_buf = pltpu.roll(buf, back, axis=0) if back else buf
    return jnp.where(rin >= s, from_x, from_buf)


def _mixer_sample_kernel(*refs, seq, chained):
    it = iter(refs)
    next(it)
    if chained:
        next(it)
    (x_ref, g_ref, win_ref, st_in_ref, qbuf_ref, cbuf_ref, cw_ref, scw_ref, alog_ref, dtb_ref,
     gn_ref, wout_ref, o_ref, st_ref, up_ref, ch_ref, mix_scr) = it
    n = x_ref.shape[0]
    nseq = n // seq
    lseq = seq.bit_length() - 1
    heads = range(GDN_HEADS)
    scw = cbuf_ref.shape[1]

    x = x_ref[...]
    u_pre, z, b_gate, c_gate, h_gate, gate_blk = _in_proj(x, g_ref, win_ref, scw)
    qbuf = qbuf_ref[...]
    conv = cw_ref[GDN_CONV - 1:GDN_CONV, :] * u_pre
    for s in range(1, GDN_CONV):
        conv = conv + cw_ref[GDN_CONV - 1 - s:GDN_CONV - s, :] * _shift_in_segments(
            u_pre, qbuf, s, GDN_CONV - 1, seq)
    qkv = _silu(conv)

    u, w, qk, qd, kt, eglast, tiles = _heads_prep(qkv, gate_blk, alog_ref[...], dtb_ref[...], seq)

    rseq = _iota2((n, GDN_DV), 0) >> lseq
    kt_t = [kt[h].T for h in heads]
    v_parts = [[] for _ in heads]
    o_parts = [[] for _ in heads]
    for s in range(nseq):
        rs = slice(s * seq, (s + 1) * seq)
        ws = [_dot(jnp.concatenate([w[h][rs], qd[h][rs]], axis=0), st_in_ref[s, h]) for h in heads]
        for h in heads:
            v_parts[h].append(u[h][rs] - ws[h][:seq])
            o_parts[h].append(ws[h][seq:])
    v_new = [jnp.concatenate(v_parts[h], axis=0) for h in heads]
    for h in heads:
        hs = slice(h * GDN_DV, (h + 1) * GDN_DV)
        o = jnp.concatenate(o_parts[h], axis=0) + jnp.concatenate(
            [_dot(qk[h][j], v_new[h][rt]) for j, rt in enumerate(tiles)], axis=0)
        mix_scr[:, hs] = _gated_out_norm(o, z[:, hs], gn_ref[...]).astype(BF16)
    for s in range(nseq):
        for h in heads:
            v_s = jnp.where(rseq == s, v_new[h], 0.0)
            st_ref[s, h] = (st_in_ref[s, h]
                            * eglast[s * seq:s * seq + 1, GATE_A_LANE + h:GATE_A_LANE + h + 1]
                            + _dot(kt_t[h], v_s))

    ch = c_gate * h_gate
    cbuf = cbuf_ref[...]
    y = scw_ref[SC_CONV - 1:SC_CONV, :] * ch
    for s in range(1, SC_CONV):
        y = y + scw_ref[SC_CONV - 1 - s:SC_CONV - s, :] * _shift_in_segments(ch, cbuf, s, SC_CONV - 1, seq)
    mix_scr[:, GDN_V:GDN_V + scw] = (b_gate * y).astype(BF16)
    o_ref[...] = x + jnp.dot(mix_scr[...], wout_ref[...], preferred_element_type=F32)
    up_ref[...] = u_pre
    ch_ref[...] = ch


def _mixer_sample(x_new, x, g, w_in, state, st_stack, qbuf, cbuf, conv_w, sconv_w, alog_v, dtb_v, g_norm,
                  w_out, layer, *, row0, n_seq, seq):
    n_all, d = x.shape
    depth = state.shape[0]
    assert seq == SUBLANES
    rows = _pick_tile(n_seq * seq, 128)
    spb = rows // seq
    assert n_seq % spb == 0 and row0 % rows == 0
    blk0 = row0 // rows
    cols = w_in.shape[2]
    scw = sconv_w.shape[2]
    chained = st_stack is not None
    kern = functools.partial(_mixer_sample_kernel, seq=seq, chained=chained)
    lead = [x_new] + ([st_stack] if chained else [])
    aliases = {0: 0, 1: 1} if chained else {0: 0}
    return pl.pallas_call(
        kern,
        grid=(n_seq // spb,),
        in_specs=[pl.BlockSpec(memory_space=pl.ANY)] * len(lead) + [
            pl.BlockSpec((rows, d), lambda i: (blk0 + i, 0)),
            pl.BlockSpec((None, 1, d), lambda i: (layer, 0, 0)),
            pl.BlockSpec((None, d, cols), lambda i: (layer, 0, 0)),
            pl.BlockSpec((None, spb, GDN_HEADS, GDN_DK, GDN_DV), lambda i: (layer, i, 0, 0, 0)),
            pl.BlockSpec((rows, QKV_WIDTH), lambda i: (i, 0)),
            pl.BlockSpec((rows, scw), lambda i: (i, 0)),
            pl.BlockSpec((None, GDN_CONV, QKV_WIDTH), lambda i: (layer, 0, 0)),
            pl.BlockSpec((None, SC_CONV, scw), lambda i: (layer, 0, 0)),
            pl.BlockSpec((None, 1, LANES), lambda i: (layer, 0, 0)),
            pl.BlockSpec((None, 1, LANES), lambda i: (layer, 0, 0)),
            pl.BlockSpec((None, 1, GDN_DV), lambda i: (layer, 0, 0)),
            pl.BlockSpec((None, GDN_V + scw, d), lambda i: (layer, 0, 0)),
        ],
        out_specs=[
            pl.BlockSpec((rows, d), lambda i: (blk0 + i, 0)),
            pl.BlockSpec((None, spb, GDN_HEADS, GDN_DK, GDN_DV), lambda i: (layer, i, 0, 0, 0)),
            pl.BlockSpec((rows, QKV_WIDTH), lambda i: (i, 0)),
            pl.BlockSpec((rows, scw), lambda i: (i, 0)),
        ],
        out_shape=[
            jax.ShapeDtypeStruct((n_all, d), F32),
            jax.ShapeDtypeStruct((depth, n_seq, GDN_HEADS, GDN_DK, GDN_DV), F32),
            jax.ShapeDtypeStruct((n_seq * seq, QKV_WIDTH), F32),
            jax.ShapeDtypeStruct((n_seq * seq, scw), F32),
        ],
        scratch_shapes=[pltpu.VMEM((rows, GDN_V + scw), BF16)],
        input_output_aliases=aliases,
        compiler_params=_params(("arbitrary",)),
        name="mixer_sample",
    )(*lead, x, g, w_in, state, qbuf, cbuf, conv_w, sconv_w, alog_v, dtb_v, g_norm, w_out)


def _softmax_rows(s):
    m = jnp.max(s, axis=-1, keepdims=True)
    p = jnp.exp(s - m)
    return p / jnp.sum(p, axis=-1, keepdims=True)


def _mem_kv_kernel(*refs, chained):
    it = iter(refs)
    if chained:
        next(it)
        next(it)
    mem_ref, wk_ref, wv_ref, kb_ref, vb_ref, k5_ref, v5_ref = it
    nh, hd = k5_ref.shape[1:]
    mb = mem_ref[...].astype(BF16)
    for w_ref, b_ref, o5_ref in ((wk_ref, kb_ref, k5_ref), (wv_ref, vb_ref, v5_ref)):
        acc = jnp.dot(mb, w_ref[...], preferred_element_type=F32)
        b_ref[...] = acc.astype(BF16)
        for h in range(nh):
            o5_ref[:, h, :] = acc[:, h * hd:(h + 1) * hd]


def _mem_kv(mem2, w_k, w_v, k_stack, v_stack, layer, *, depth, batch, mem_len):
    d = mem2.shape[1]
    hd = d // X_HEADS
    chained = k_stack is not None
    lead = [k_stack, v_stack] if chained else []
    kern = functools.partial(_mem_kv_kernel, chained=chained)
    stack_shape = jax.ShapeDtypeStruct((depth, batch, mem_len, X_HEADS, hd), F32)
    spec5 = pl.BlockSpec((None, None, mem_len, X_HEADS, hd), lambda b: (layer, b, 0, 0, 0))
    return pl.pallas_call(
        kern,
        grid=(batch,),
        in_specs=[pl.BlockSpec(memory_space=pl.ANY)] * len(lead) + [
            pl.BlockSpec((mem_len, d), lambda b: (b, 0)),
            pl.BlockSpec((None, d, d), lambda b: (layer, 0, 0)),
            pl.BlockSpec((None, d, d), lambda b: (layer, 0, 0)),
        ],
        out_specs=[
            pl.BlockSpec((mem_len, d), lambda b: (b, 0)),
            pl.BlockSpec((mem_len, d), lambda b: (b, 0)),
            spec5, spec5,
        ],
        out_shape=[
            jax.ShapeDtypeStruct((batch * mem_len, d), BF16),
            jax.ShapeDtypeStruct((batch * mem_len, d), BF16),
            stack_shape, stack_shape,
        ],
        input_output_aliases={0: 2, 1: 3} if chained else {},
        compiler_params=_params(("arbitrary",)),
        name="mem_kv",
    )(*lead, mem2, w_k, w_v)


def _xattn_prompt_kernel(x_ref, g_ref, wq_ref, k_ref, v_ref, wo_ref, o_ref, att_scr):
    hd = x_ref.shape[1] // X_HEADS
    x = x_ref[...]
    xb = _rms(x, g_ref[...]).astype(BF16)
    heads = [slice(h * hd, (h + 1) * hd) for h in range(X_HEADS)]
    q = [jnp.dot(xb, wq_ref[:, hs], preferred_element_type=F32).astype(BF16) for hs in heads]
    s = [_dot_nt(q[h], k_ref[:, hs]) * (hd ** -0.5) for h, hs in enumerate(heads)]
    p = [_softmax_rows(sh).astype(BF16) for sh in s]
    for h, hs in enumerate(heads):
        att_scr[:, hs] = _dot(p[h], v_ref[:, hs]).astype(BF16)
    o_ref[...] = x + jnp.dot(att_scr[...], wo_ref[...], preferred_element_type=F32)


def _xattn_prompt(x, g, w_q, mk, mv, w_o, layer, *, batch, seq, mem_len):
    n_all, d = x.shape
    tq = _pick_tile(seq, ROW_TILE)
    nq = seq // tq
    return pl.pallas_call(
        _xattn_prompt_kernel,
        grid=(batch, nq),
        in_specs=[
            pl.BlockSpec((tq, d), lambda b, t: (b * nq + t, 0)),
            pl.BlockSpec((None, 1, d), lambda b, t: (layer, 0, 0)),
            pl.BlockSpec((None, d, d), lambda b, t: (layer, 0, 0)),
            pl.BlockSpec((mem_len, d), lambda b, t: (b, 0)),
            pl.BlockSpec((mem_len, d), lambda b, t: (b, 0)),
            pl.BlockSpec((None, d, d), lambda b, t: (layer, 0, 0)),
        ],
        out_specs=pl.BlockSpec((tq, d), lambda b, t: (b * nq + t, 0)),
        out_shape=jax.ShapeDtypeStruct((n_all, d), F32),
        scratch_shapes=[pltpu.VMEM((tq, d), BF16)],
        compiler_params=_params(("arbitrary", "arbitrary")),
        name="xattn_prompt",
    )(x, g, w_q, mk, mv, w_o)


def _attn_sample_kernel(q_ref, k_ref, v_ref, o_ref, *, seq, spb):
    mem_len, nh, hd = k_ref.shape[1:]
    lseq = seq.bit_length() - 1
    rows_h = _iota2((nh * seq, mem_len * nh), 0) >> lseq
    cols_h = _iota2((nh * seq, mem_len * nh), 1) & (nh - 1)
    own_head = rows_h == cols_h
    for s in range(spb):
        rs = slice(s * seq, (s + 1) * seq)
        q = q_ref[rs, :]
        q_all = jnp.concatenate([q[:, h * hd:(h + 1) * hd] for h in range(nh)], axis=0)
        k_all = k_ref[s].reshape(mem_len * nh, hd)
        v_all = v_ref[s].reshape(mem_len * nh, hd)
        sc = jnp.where(own_head, _dot_nt(q_all, k_all) * (hd ** -0.5), -jnp.inf)
        o_all = _dot(_softmax_rows(sc), v_all)
        for h in range(nh):
            o_ref[rs, h * hd:(h + 1) * hd] = o_all[h * seq:(h + 1) * seq].astype(o_ref.dtype)


def _attn_sample(q, cache_k, cache_v, layer, *, n_seq, seq):
    d = q.shape[1]
    _, _, mem_len, nh, hd = cache_k.shape
    assert nh & (nh - 1) == 0
    spb = 4
    assert n_seq % spb == 0
    rows = spb * seq
    kern = functools.partial(_attn_sample_kernel, seq=seq, spb=spb)
    return pl.pallas_call(
        kern,
        grid=(n_seq // spb,),
        in_specs=[
            pl.BlockSpec((rows, d), lambda i: (i, 0)),
            pl.BlockSpec((None, spb, mem_len, nh, hd), lambda i: (layer, i, 0, 0, 0)),
            pl.BlockSpec((None, spb, mem_len, nh, hd), lambda i: (layer, i, 0, 0, 0)),
        ],
        out_specs=pl.BlockSpec((rows, d), lambda i: (i, 0)),
        out_shape=jax.ShapeDtypeStruct(q.shape, BF16),
        compiler_params=_params(("arbitrary",)),
        name="attn_sample",
    )(q, cache_k, cache_v)


def _gate_lane_vector(v, lane0):
    depth, n = v.shape
    out = jnp.zeros((depth, 1, LANES), F32)
    return out.at[:, 0, lane0:lane0 + n].set(v.astype(F32))


def kernel(x_prompt, x_sample, mem_prompt, state_gdn, state_qkv_conv, state_short_conv, cache_mem_k, cache_mem_v, g_ffn1, w_ffn1_gu, w_ffn1_down, g_mix, w_in, conv_qkv_w, a_log, dt_bias, g_gdn_out, sconv_w, w_out, g_xattn, w_xq, w_xk, w_xv, w_xo, g_ffn2, w_ffn2_gu, w_ffn2_down, g_final):
    batch, seq, d = x_prompt.shape
    dec_batch, dec_seq, _ = x_sample.shape
    depth = w_in.shape[0]
    mem_len = mem_prompt.shape[1]
    scw = sconv_w.shape[2]
    n_p = batch * seq
    n_s = dec_batch * dec_seq
    off_beta = QKV_WIDTH + GDN_V
    off_sc = off_beta + 2 * GDN_HEADS

    w_in_p = jnp.concatenate(
        [w_in[:, :, :off_beta], w_in[:, :, off_sc:], w_in[:, :, off_beta:off_sc],
         jnp.zeros((depth, d, LANES - 2 * GDN_HEADS), w_in.dtype)], axis=2).astype(BF16)
    wgu1, wd1 = w_ffn1_gu.astype(BF16), w_ffn1_down.astype(BF16)
    wgu2, wd2 = w_ffn2_gu.astype(BF16), w_ffn2_down.astype(BF16)
    w_out_b, w_xq_b, w_xo_b = w_out.astype(BF16), w_xq.astype(BF16), w_xo.astype(BF16)
    w_xk_b, w_xv_b = w_xk.astype(BF16), w_xv.astype(BF16)
    alog_v = _gate_lane_vector(a_log, GATE_A_LANE)
    dtb_v = _gate_lane_vector(dt_bias, GATE_A_LANE)
    g1, gm, gx, g2 = (t.reshape(depth, 1, d) for t in (g_ffn1, g_mix, g_xattn, g_ffn2))
    gn = g_gdn_out.reshape(depth, 1, GDN_DV)
    gf = g_final.reshape(1, d)
    mem2 = mem_prompt.reshape(batch * mem_len, d)
    qbuf_all = jnp.pad(state_qkv_conv, ((0, 0), (0, 0), (0, dec_seq - (GDN_CONV - 1)), (0, 0)))
    cbuf_all = jnp.pad(state_short_conv, ((0, 0), (0, 0), (0, dec_seq - (SC_CONV - 1)), (0, 0)))

    p_s, p_qb, p_sb, s_qb, s_sb = [], [], [], [], []
    st_stack = k_stack = v_stack = None
    x = None
    for l in range(depth):
        xs = [x_prompt.reshape(n_p, d), x_sample.reshape(n_s, d)] if l == 0 else [x]
        x = _ffn(xs, g1, wgu1, wd1, gf, l, final_norm=False)

        x1, st_p, qb_p, chb_p = _mixer_prompt(x, gm, w_in_p, conv_qkv_w, sconv_w, alog_v, dtb_v, gn,
                                              w_out_b, l, batch=batch, seq=seq)
        x1, st_stack, up_s, ch_s = _mixer_sample(
            x1, x, gm, w_in_p, state_gdn, st_stack, qbuf_all[l].reshape(n_s, QKV_WIDTH),
            cbuf_all[l].reshape(n_s, scw), conv_qkv_w, sconv_w, alog_v, dtb_v, gn, w_out_b, l,
            row0=n_p, n_seq=dec_batch, seq=dec_seq)

        mk, mv, k_stack, v_stack = _mem_kv(mem2, w_xk_b, w_xv_b, k_stack, v_stack, l,
                                           depth=depth, batch=batch, mem_len=mem_len)
        x2 = _xattn_prompt(x1, gx, w_xq_b, mk, mv, w_xo_b, l, batch=batch, seq=seq, mem_len=mem_len)
        q_s = _matmul(x1, w_xq_b, l, n_rows=n_s, x_row0=n_p, g=gx, out_dtype=BF16, name="xq_sample")
        att_s = _attn_sample(q_s, cache_mem_k, cache_mem_v, l, n_seq=dec_batch, seq=dec_seq)
        x2 = _matmul(att_s, w_xo_b, l, n_rows=n_s, res=x1, res_row0=n_p, into=x2, out_row0=n_p,
                     name="xo_sample")

        if l == depth - 1:
            y_prompt, y_sample = _ffn([x2], g2, wgu2, wd2, gf, l, final_norm=True, split_rows=n_p)
        else:
            x = _ffn([x2], g2, wgu2, wd2, gf, l, final_norm=False)

        p_s.append(st_p)
        p_qb.append(qb_p[:, SUBLANES - (GDN_CONV - 1):, :])
        p_sb.append(chb_p[:, SUBLANES - (SC_CONV - 1):, :])
        s_qb.append(up_s.reshape(dec_batch, dec_seq, QKV_WIDTH)[:, dec_seq - (GDN_CONV - 1):, :])
        s_sb.append(ch_s.reshape(dec_batch, dec_seq, scw)[:, dec_seq - (SC_CONV - 1):, :])

    return (y_prompt.reshape(batch, seq, d), y_sample.reshape(dec_batch, dec_seq, d),
            jnp.stack(p_s), jnp.stack(p_qb), jnp.stack(p_sb), k_stack, v_stack,
            st_stack, jnp.stack(s_qb), jnp.stack(s_sb))
```

```python
import functools

import jax
import jax.numpy as jnp
from jax import lax
from jax.experimental import pallas as pl
from jax.experimental.pallas import tpu as pltpu

F32 = jnp.float32
BF16 = jnp.bfloat16

GDN_HEADS = 4
GDN_DK = 128
GDN_DV = 128
GDN_QK = GDN_HEADS * GDN_DK
GDN_V = GDN_HEADS * GDN_DV
QKV_WIDTH = 2 * GDN_QK + GDN_V
GDN_CONV = 4
SC_CONV = 3
CHUNK = 64
MAT_TILE = 128
STAGE_ROWS = 256
X_HEADS = 4
RMS_EPS = 1e-6
LANES = 128
SUBLANES = 8
VMEM_LIMIT = 56 * 1024 * 1024
ROW_TILE = 512

COL_Z = QKV_WIDTH
COL_SCB = COL_Z + GDN_V
GATE_BETA_LANE = 0
GATE_A_LANE = GDN_HEADS


def _params(sem):
    return pltpu.CompilerParams(dimension_semantics=sem, vmem_limit_bytes=VMEM_LIMIT)


def _pick_tile(n, pref):
    t = min(pref, n)
    while n % t:
        t -= SUBLANES
    assert t > 0 and t % SUBLANES == 0
    return t


def _silu(x):
    return x * jax.nn.sigmoid(x)


def _rms(x, g):
    return x * lax.rsqrt(jnp.mean(x * x, axis=-1, keepdims=True) + RMS_EPS) * g


def _dot(a, b):
    return jnp.dot(a.astype(BF16), b.astype(BF16), preferred_element_type=F32)


def _dot_nt(a, b):
    return lax.dot_general(a.astype(BF16), b.astype(BF16), (((1,), (1,)), ((), ())),
                           preferred_element_type=F32)


def _iota2(shape, dim):
    return lax.broadcasted_iota(jnp.int32, shape, dim)


def _ffn_kernel(*refs, d_ff, col_chunk, final_norm, n_in, n_out, split_tile):
    x_refs = refs[:n_in]
    g_ref, wgu_ref, wd_ref, gf_ref = refs[n_in:n_in + 4]
    o_refs = refs[n_in + 4:n_in + 4 + n_out]
    h_ref = refs[n_in + 4 + n_out]
    i = pl.program_id(0)
    if n_in == 2:
        x = jnp.where(i < split_tile, x_refs[0][...], x_refs[1][...])
    else:
        x = x_refs[0][...]
    xb = _rms(x, g_ref[...]).astype(BF16)
    for c in range(d_ff // col_chunk):
        lo = c * col_chunk
        gate = jnp.dot(xb, wgu_ref[:, lo:lo + col_chunk], preferred_element_type=F32)
        up = jnp.dot(xb, wgu_ref[:, d_ff + lo:d_ff + lo + col_chunk], preferred_element_type=F32)
        h_ref[:, lo:lo + col_chunk] = (_silu(gate) * up).astype(BF16)
    y = x + 0.5 * jnp.dot(h_ref[...], wd_ref[...], preferred_element_type=F32)
    if final_norm:
        y = _rms(y, gf_ref[...])
    if n_out == 2:
        @pl.when(i < split_tile)
        def _():
            o_refs[0][...] = y

        @pl.when(i >= split_tile)
        def _():
            o_refs[1][...] = y
    else:
        o_refs[0][...] = y


def _ffn(xs, g, wgu, wd, g_final, layer, *, final_norm, split_rows=None):
    d = xs[0].shape[1]
    n = sum(a.shape[0] for a in xs)
    d_ff = wd.shape[1]
    tm = ROW_TILE
    col_chunk = 256
    assert d_ff % col_chunk == 0 and all(a.shape[0] % tm == 0 for a in xs)
    if len(xs) == 2:
        split_tile = xs[0].shape[0] // tm
        x_specs = [pl.BlockSpec((tm, d), lambda i: (jnp.minimum(i, split_tile - 1), 0)),
                   pl.BlockSpec((tm, d), lambda i: (jnp.maximum(i - split_tile, 0), 0))]
    else:
        split_tile = None
        x_specs = [pl.BlockSpec((tm, d), lambda i: (i, 0))]
    if split_rows is not None:
        assert split_rows % tm == 0
        split_tile = split_rows // tm
        out_specs = [pl.BlockSpec((tm, d), lambda i: (jnp.minimum(i, split_tile - 1), 0)),
                     pl.BlockSpec((tm, d), lambda i: (jnp.maximum(i - split_tile, 0), 0))]
        out_shape = [jax.ShapeDtypeStruct((split_rows, d), F32),
                     jax.ShapeDtypeStruct((n - split_rows, d), F32)]
    else:
        out_specs = [pl.BlockSpec((tm, d), lambda i: (i, 0))]
        out_shape = [jax.ShapeDtypeStruct((n, d), F32)]
    kern = functools.partial(_ffn_kernel, d_ff=d_ff, col_chunk=col_chunk, final_norm=final_norm,
                             n_in=len(xs), n_out=len(out_specs), split_tile=split_tile)
    out = pl.pallas_call(
        kern,
        grid=(n // tm,),
        in_specs=x_specs + [
            pl.BlockSpec((None, 1, d), lambda i: (layer, 0, 0)),
            pl.BlockSpec((None, d, 2 * d_ff), lambda i: (layer, 0, 0)),
            pl.BlockSpec((None, d_ff, d), lambda i: (layer, 0, 0)),
            pl.BlockSpec((1, d), lambda i: (0, 0)),
        ],
        out_specs=out_specs,
        out_shape=out_shape,
        scratch_shapes=[pltpu.VMEM((tm, d_ff), BF16)],
        compiler_params=_params(("arbitrary",)),
        name="ffn",
    )(*xs, g, wgu, wd, g_final)
    return out if split_rows is not None else out[0]


def _mm_kernel(*refs, rms, residual, aliased):
    it = iter(refs)
    if aliased:
        next(it)
    x_ref = next(it)
    g_ref = next(it) if rms else None
    w_ref = next(it)
    r_ref = next(it) if residual else None
    o_ref = next(it)
    x = x_ref[...]
    if rms:
        x = _rms(x, g_ref[...])
    acc = jnp.dot(x.astype(BF16), w_ref[...], preferred_element_type=F32)
    if residual:
        acc = acc + r_ref[...]
    o_ref[...] = acc.astype(o_ref.dtype)


def _matmul(x, w, layer, *, n_rows, x_row0=0, g=None, res=None, res_row0=0, into=None, out_row0=0,
            out_dtype=F32, name="mm"):
    k = x.shape[1]
    n_out = w.shape[2]
    tm = _pick_tile(n_rows, ROW_TILE)
    assert x_row0 % tm == 0 and res_row0 % tm == 0 and out_row0 % tm == 0
    xb0, rb0, ob0 = x_row0 // tm, res_row0 // tm, out_row0 // tm
    args, specs = [], []
    if into is not None:
        args.append(into)
        specs.append(pl.BlockSpec(memory_space=pl.ANY))
    args.append(x)
    specs.append(pl.BlockSpec((tm, k), lambda i: (xb0 + i, 0)))
    if g is not None:
        args.append(g)
        specs.append(pl.BlockSpec((None, 1, k), lambda i: (layer, 0, 0)))
    args.append(w)
    specs.append(pl.BlockSpec((None, k, n_out), lambda i: (layer, 0, 0)))
    if res is not None:
        args.append(res)
        specs.append(pl.BlockSpec((tm, n_out), lambda i: (rb0 + i, 0)))
    kern = functools.partial(_mm_kernel, rms=g is not None, residual=res is not None,
                             aliased=into is not None)
    out_shape = (jax.ShapeDtypeStruct(into.shape, into.dtype) if into is not None
                 else jax.ShapeDtypeStruct((n_rows, n_out), out_dtype))
    return pl.pallas_call(
        kern,
        grid=(n_rows // tm,),
        in_specs=specs,
        out_specs=pl.BlockSpec((tm, n_out), lambda i: (ob0 + i, 0)),
        out_shape=out_shape,
        input_output_aliases={0: 0} if into is not None else {},
        compiler_params=_params(("arbitrary",)),
        name=name,
    )(*args)


_DONE = object()


def _co_schedule(main, filler):
    for n in main:
        for _ in range(n or 0):
            next(filler, _DONE)
    for _ in filler:
        pass


def _run(stream):
    while True:
        try:
            next(stream)
        except StopIteration as stop:
            return stop.value


def _unit_lower_inverse_steps(a_list, merge_levels, fill=None):
    n = a_list[0].shape[0]
    r = _iota2((n, n), 0)
    c = _iota2((n, n), 1)
    eye = (r == c).astype(F32)
    blk8 = (r >> 3) == (c >> 3)
    d = [jnp.where(blk8, a, 0.0) for a in a_list]
    d2 = [_dot(x, x) for x in d]
    yield fill
    p = [_dot(eye - x, eye + y) for x, y in zip(d, d2)]
    d4 = [_dot(y, y) for y in d2]
    yield fill
    t = [_dot(x, eye + y) for x, y in zip(p, d4)]
    yield fill
    for lvl in merge_levels:
        mask = ((r >> (lvl + 1)) == (c >> (lvl + 1))) & ((r >> lvl) != (c >> lvl))
        lt = [_dot(jnp.where(mask, a, 0.0), x) for a, x in zip(a_list, t)]
        yield fill
        t = [x - _dot(x, y) for x, y in zip(t, lt)]
        yield fill
    return t


def _unit_lower_inverse(a_list, merge_levels):
    return _run(_unit_lower_inverse_steps(a_list, merge_levels))


def _l2n(t):
    return t * lax.rsqrt(jnp.sum(t * t, axis=-1, keepdims=True) + 1e-6)


def _segment_cumsum(g, seg):
    rin = _iota2(g.shape, 0) & (seg - 1)
    sh = 1
    while sh < seg:
        g = g + jnp.where(rin >= sh, pltpu.roll(g, sh, axis=0), 0.0)
        sh *= 2
    return g


def _segment_last(gc, seg):
    n = gc.shape[0]
    rin = _iota2(gc.shape, 0) & (seg - 1)
    x = jnp.where(rin == seg - 1, gc, 0.0)
    sh = 1
    while sh < seg:
        x = x + jnp.where(rin + sh <= seg - 1, pltpu.roll(x, n - sh, axis=0), 0.0)
        sh *= 2
    return x


def _gates(gate_blk, alog, dtb, seg):
    beta = jax.nn.sigmoid(gate_blk)
    xg = gate_blk + dtb
    softplus = jnp.maximum(xg, 0.0) + jnp.log1p(jnp.exp(-jnp.abs(xg)))
    g = -jnp.exp(alog) * softplus
    gc = _segment_cumsum(g, seg)
    return beta, gc


def _heads_prep(qkv, gate_blk, alog, dtb, seg):
    n = qkv.shape[0]
    mt = min(n, MAT_TILE)
    assert n % mt == 0 and mt % seg == 0
    tiles = [slice(j * mt, (j + 1) * mt) for j in range(n // mt)]
    lg = seg.bit_length() - 1
    beta, gc = _gates(gate_blk, alog, dtb, seg)
    egc = jnp.exp(gc)
    glast = _segment_last(gc, seg)
    ekt = jnp.exp(glast - gc)
    eglast = jnp.exp(glast)
    gc_t = gc.T
    r = _iota2((mt, mt), 0)
    c = _iota2((mt, mt), 1)
    same = (r >> lg) == (c >> lg)
    tri = (r >= c) & same
    strict = (r > c) & same
    heads = range(GDN_HEADS)
    q = [_l2n(qkv[:, h * GDN_DK:(h + 1) * GDN_DK]) * (GDN_DK ** -0.5) for h in heads]
    k = [_l2n(qkv[:, GDN_QK + h * GDN_DK:GDN_QK + (h + 1) * GDN_DK]) for h in heads]
    v = [qkv[:, 2 * GDN_QK + h * GDN_DV:2 * GDN_QK + (h + 1) * GDN_DV] for h in heads]
    bh = [beta[:, GATE_BETA_LANE + h:GATE_BETA_LANE + h + 1] for h in heads]
    eg = [egc[:, GATE_A_LANE + h:GATE_A_LANE + h + 1] for h in heads]
    kb = [k[h] * bh[h] for h in heads]
    rhs = [jnp.concatenate([v[h] * bh[h], kb[h] * eg[h]], axis=1) for h in heads]
    dec = [[None] * len(tiles) for _ in heads]
    for h in heads:
        al = GATE_A_LANE + h
        for j, rt in enumerate(tiles):
            diff = gc[rt, al:al + 1] - gc_t[al:al + 1, rt]
            dec[h][j] = jnp.where(tri, jnp.exp(jnp.where(tri, diff, 0.0)), 0.0)
    systems = [(h, j) for h in heads for j in range(len(tiles))]
    a = [jnp.where(strict, _dot_nt(kb[h][tiles[j]], k[h][tiles[j]]) * dec[h][j], 0.0)
         for h, j in systems]
    tinv = _unit_lower_inverse(a, merge_levels=tuple(range(3, lg)))
    uw = [_dot(t, rhs[h][tiles[j]]) for t, (h, j) in zip(tinv, systems)]
    uw = [jnp.concatenate(uw[h * len(tiles):(h + 1) * len(tiles)], axis=0) for h in heads]
    u = [x[:, :GDN_DV] for x in uw]
    w = [x[:, GDN_DV:] for x in uw]
    qk = [[_dot_nt(q[h][rt], k[h][rt]) * dec[h][j] for j, rt in enumerate(tiles)] for h in heads]
    qd = [q[h] * eg[h] for h in heads]
    kt = [k[h] * ekt[:, GATE_A_LANE + h:GATE_A_LANE + h + 1] for h in heads]
    return u, w, qk, qd, kt, eglast, tiles


def _gated_out_norm(o, z, gn):
    return _rms(o, gn) * _silu(z)


def _in_proj_cols(x, g_ref, win_ref):
    xb = _rms(x, g_ref[...]).astype(BF16)

    def cols(lo, width):
        return jnp.dot(xb, win_ref[:, lo:lo + width], preferred_element_type=F32)

    return cols


def _in_proj(x, g_ref, win_ref, scw):
    cols = _in_proj_cols(x, g_ref, win_ref)
    return (cols(0, QKV_WIDTH), cols(COL_Z, GDN_V), cols(COL_SCB, scw), cols(COL_SCB + scw, scw),
            cols(COL_SCB + 2 * scw, scw), cols(COL_SCB + 3 * scw, LANES))


OPS_Q, OPS_K, OPS_KB, OPS_QD, OPS_RHS = 0, GDN_QK, 2 * GDN_QK, 3 * GDN_QK, 4 * GDN_QK
OPS_WIDTH = 4 * GDN_QK + GDN_HEADS * 2 * GDN_DV
FILL_PER_CHUNK = 1
PROJ_SLAB = 256


def _mixer_pipe_kernel(xa_ref, xb_ref, g_ref, win_ref, cw_ref, scw_ref, alog_ref, dtb_ref, gn_ref, wout_ref,
                       o_ref, st_ref, qb_ref, chb_ref,
                       s_scr, extq, extc, mix_scr, ops_new, ops_old, kt_new, kt_old, xn_new, xn_old,
                       gate_new, gate_old, gct_new, gct_old, *, lt, nt, n_tiles):
    i = pl.program_id(0)
    ta = jnp.minimum(i, n_tiles - 1) % nt
    tb = jnp.maximum(i - 1, 0) % nt
    heads = range(GDN_HEADS)
    scw = extc.shape[1]
    mt = min(lt, MAT_TILE)
    tiles = [slice(j * mt, (j + 1) * mt) for j in range(lt // mt)]
    lg = CHUNK.bit_length() - 1

    @pl.when(i == 0)
    def _():
        for ref in (ops_old, kt_old, xn_old, gate_old, gct_old):
            ref[...] = jnp.zeros_like(ref)

    @pl.when(ta == 0)
    def _():
        extq[0:SUBLANES, :] = jnp.zeros((SUBLANES, QKV_WIDTH), F32)

    @pl.when(tb == 0)
    def _():
        s_scr[...] = jnp.zeros_like(s_scr)
        extc[0:SUBLANES, :] = jnp.zeros((SUBLANES, scw), F32)

    late = {}
    pending = [("b", COL_SCB), ("c", COL_SCB + scw), ("h", COL_SCB + 2 * scw)]
    ch_tail = []

    def cols_b(lo, width):
        return jnp.dot(xn_old[...], win_ref[:, lo:lo + width], preferred_element_type=F32)

    def stage_a():
        xn = _rms(xa_ref[...], g_ref[...]).astype(BF16)
        xn_new[...] = xn
        gate_blk = jnp.dot(xn, win_ref[:, COL_SCB + 3 * scw:COL_SCB + 3 * scw + LANES],
                           preferred_element_type=F32)
        beta, gc = _gates(gate_blk, alog_ref[...], dtb_ref[...], CHUNK)
        egc = jnp.exp(gc)
        glast = _segment_last(gc, CHUNK)
        ekt = jnp.exp(glast - gc)
        gate_new[:, 0:LANES] = gc
        gate_new[:, LANES:2 * LANES] = jnp.exp(glast)
        gct_new[...] = gc.T
        yield
        act = []
        for slab in range(QKV_WIDTH // PROJ_SLAB):
            sl = slice(slab * PROJ_SLAB, (slab + 1) * PROJ_SLAB)
            extq[SUBLANES:SUBLANES + lt, sl] = jnp.dot(xn, win_ref[:, sl], preferred_element_type=F32)
            for gcol in range(slab * PROJ_SLAB // LANES, (slab + 1) * PROJ_SLAB // LANES):
                cs = slice(gcol * LANES, (gcol + 1) * LANES)
                conv = cw_ref[GDN_CONV - 1:GDN_CONV, cs] * extq[SUBLANES:SUBLANES + lt, cs]
                for s in range(1, GDN_CONV):
                    conv = conv + cw_ref[GDN_CONV - 1 - s:GDN_CONV - s, cs] * extq[SUBLANES - s:SUBLANES - s + lt, cs]
                act.append(_silu(conv))
            yield
        for h in heads:
            hs = slice(h * GDN_DK, (h + 1) * GDN_DK)
            bh = beta[:, GATE_BETA_LANE + h:GATE_BETA_LANE + h + 1]
            eg = egc[:, GATE_A_LANE + h:GATE_A_LANE + h + 1]
            q = _l2n(act[h]) * (GDN_DK ** -0.5)
            k = _l2n(act[GDN_HEADS + h])
            kb = k * bh
            ops_new[:, OPS_Q + h * GDN_DK:OPS_Q + (h + 1) * GDN_DK] = q.astype(BF16)
            ops_new[:, OPS_K + h * GDN_DK:OPS_K + (h + 1) * GDN_DK] = k.astype(BF16)
            ops_new[:, OPS_KB + h * GDN_DK:OPS_KB + (h + 1) * GDN_DK] = kb.astype(BF16)
            ops_new[:, OPS_QD + h * GDN_DK:OPS_QD + (h + 1) * GDN_DK] = (q * eg).astype(BF16)
            lo = OPS_RHS + h * 2 * GDN_DV
            ops_new[:, lo:lo + GDN_DV] = (act[2 * GDN_HEADS + h] * bh).astype(BF16)
            ops_new[:, lo + GDN_DV:lo + 2 * GDN_DV] = (kb * eg).astype(BF16)
            kt_new[:, hs] = k * ekt[:, GATE_A_LANE + h:GATE_A_LANE + h + 1]
            yield

    def stage_b():
        late["z"] = cols_b(COL_Z, GDN_V)
        yield 1
        r = _iota2((mt, mt), 0)
        c = _iota2((mt, mt), 1)
        same = (r >> lg) == (c >> lg)
        tri = (r >= c) & same
        strict = (r > c) & same
        systems = [(h, j) for h in heads for j in range(len(tiles))]

        def op(lo, h, rows, width=GDN_DK):
            return ops_old[rows, lo + h * width:lo + (h + 1) * width]

        dec = []
        for h, j in systems:
            al = GATE_A_LANE + h
            diff = gate_old[tiles[j], al:al + 1] - gct_old[al:al + 1, tiles[j]]
            dec.append(jnp.where(tri, jnp.exp(jnp.where(tri, diff, 0.0)), 0.0))
        yield
        a = [jnp.where(strict, _dot_nt(op(OPS_KB, h, tiles[j]), op(OPS_K, h, tiles[j])) * dk, 0.0)
             for (h, j), dk in zip(systems, dec)]
        yield
        inverse = _unit_lower_inverse_steps(a, merge_levels=tuple(range(3, lg)), fill=1)
        while True:
            try:
                tag = next(inverse)
            except StopIteration as stop:
                tinv = stop.value
                break
            if pending:
                name, lo = pending.pop(0)
                late[name] = cols_b(lo, scw)
            yield tag
        while pending:
            name, lo = pending.pop(0)
            late[name] = cols_b(lo, scw)
        uw = [_dot(tk, op(OPS_RHS, h, tiles[j], 2 * GDN_DV)) for (h, j), tk in zip(systems, tinv)]
        uw = [jnp.concatenate(uw[h * len(tiles):(h + 1) * len(tiles)], axis=0) for h in heads]
        yield 1
        qk = [_dot_nt(op(OPS_Q, h, tiles[j]), op(OPS_K, h, tiles[j])) * dk
              for (h, j), dk in zip(systems, dec)]
        yield 1

        def short_conv_group(gcol):
            cs = slice(gcol * LANES, (gcol + 1) * LANES)
            ch = late["c"][:, cs] * late["h"][:, cs]
            extc[SUBLANES:SUBLANES + lt, cs] = ch
            y = scw_ref[SC_CONV - 1:SC_CONV, cs] * ch
            for s in range(1, SC_CONV):
                y = y + scw_ref[SC_CONV - 1 - s:SC_CONV - s, cs] * extc[SUBLANES - s:SUBLANES - s + lt, cs]
            mix_scr[:, GDN_V + gcol * LANES:GDN_V + (gcol + 1) * LANES] = (late["b"][:, cs] * y).astype(BF16)
            ch_tail.append(ch[lt - SUBLANES:lt, :])

        sc_groups = list(range(scw // LANES))
        n_chunks = lt // CHUNK
        s_cur = [s_scr[h] for h in heads]
        v_parts = [[] for _ in heads]
        o_parts = [[] for _ in heads]
        for cc in range(n_chunks):
            row0 = cc * CHUNK
            rs = slice(row0, row0 + CHUNK)
            ws = [_dot(jnp.concatenate([uw[h][rs, GDN_DV:], op(OPS_QD, h, rs)], axis=0), s_cur[h])
                  for h in heads]
            vn = [uw[h][rs, :GDN_DV] - ws[h][:CHUNK] for h in heads]
            s_cur = [s_cur[h] * gate_old[row0:row0 + 1, LANES + GATE_A_LANE + h:LANES + GATE_A_LANE + h + 1]
                     + _dot(kt_old[rs, h * GDN_DK:(h + 1) * GDN_DK].T, vn[h]) for h in heads]
            for h in heads:
                v_parts[h].append(vn[h])
                o_parts[h].append(ws[h][CHUNK:])
            for _ in range(-(-len(sc_groups) // (n_chunks - cc))):
                short_conv_group(sc_groups.pop(0))
            yield FILL_PER_CHUNK
        for h in heads:
            s_scr[h] = s_cur[h]
            v_new = jnp.concatenate(v_parts[h], axis=0)
            o = jnp.concatenate(o_parts[h], axis=0) + jnp.concatenate(
                [_dot(qk[h * len(tiles) + j], v_new[rt]) for j, rt in enumerate(tiles)], axis=0)
            hs = slice(h * GDN_DV, (h + 1) * GDN_DV)
            mix_scr[:, hs] = _gated_out_norm(o, late["z"][:, hs], gn_ref[...]).astype(BF16)
            yield

    _co_schedule(stage_b(), stage_a())

    o_ref[...] = xb_ref[...] + jnp.dot(mix_scr[...], wout_ref[...], preferred_element_type=F32)

    u_tail = extq[lt:lt + SUBLANES, :]
    c_tail = jnp.concatenate(ch_tail, axis=1)
    extq[0:SUBLANES, :] = u_tail
    extc[0:SUBLANES, :] = c_tail
    for new, old in ((ops_new, ops_old), (kt_new, kt_old), (xn_new, xn_old), (gate_new, gate_old),
                     (gct_new, gct_old)):
        old[...] = new[...]

    @pl.when(ta == nt - 1)
    def _():
        qb_ref[...] = u_tail

    @pl.when(tb == nt - 1)
    def _():
        st_ref[...] = s_scr[...]
        chb_ref[...] = c_tail


def _mixer_pipe(x, g, w_in, conv_w, sconv_w, alog_v, dtb_v, g_norm, w_out, layer, *, batch, seq):
    n_all, d = x.shape
    lt = _pick_tile(seq, STAGE_ROWS)
    assert lt % CHUNK == 0
    nt = seq // lt
    n_tiles = batch * nt
    cols = w_in.shape[2]
    scw = sconv_w.shape[2]
    kern = functools.partial(_mixer_pipe_kernel, lt=lt, nt=nt, n_tiles=n_tiles)

    def tile_a(i):
        return jnp.minimum(i, n_tiles - 1)

    def tile_b(i):
        return jnp.maximum(i - 1, 0)

    return pl.pallas_call(
        kern,
        grid=(n_tiles + 1,),
        in_specs=[
            pl.BlockSpec((lt, d), lambda i: (tile_a(i), 0)),
            pl.BlockSpec((lt, d), lambda i: (tile_b(i), 0)),
            pl.BlockSpec((None, 1, d), lambda i: (layer, 0, 0)),
            pl.BlockSpec((None, d, cols), lambda i: (layer, 0, 0)),
            pl.BlockSpec((None, GDN_CONV, QKV_WIDTH), lambda i: (layer, 0, 0)),
            pl.BlockSpec((None, SC_CONV, scw), lambda i: (layer, 0, 0)),
            pl.BlockSpec((None, 1, LANES), lambda i: (layer, 0, 0)),
            pl.BlockSpec((None, 1, LANES), lambda i: (layer, 0, 0)),
            pl.BlockSpec((None, 1, GDN_DV), lambda i: (layer, 0, 0)),
            pl.BlockSpec((None, GDN_V + scw, d), lambda i: (layer, 0, 0)),
        ],
        out_specs=[
            pl.BlockSpec((lt, d), lambda i: (tile_b(i), 0)),
            pl.BlockSpec((None, GDN_HEADS, GDN_DK, GDN_DV), lambda i: (tile_b(i) // nt, 0, 0, 0)),
            pl.BlockSpec((None, SUBLANES, QKV_WIDTH), lambda i: (tile_a(i) // nt, 0, 0)),
            pl.BlockSpec((None, SUBLANES, scw), lambda i: (tile_b(i) // nt, 0, 0)),
        ],
        out_shape=[
            jax.ShapeDtypeStruct((n_all, d), F32),
            jax.ShapeDtypeStruct((batch, GDN_HEADS, GDN_DK, GDN_DV), F32),
            jax.ShapeDtypeStruct((batch, SUBLANES, QKV_WIDTH), F32),
            jax.ShapeDtypeStruct((batch, SUBLANES, scw), F32),
        ],
        scratch_shapes=[
            pltpu.VMEM((GDN_HEADS, GDN_DK, GDN_DV), F32),
            pltpu.VMEM((lt + SUBLANES, QKV_WIDTH), F32),
            pltpu.VMEM((lt + SUBLANES, scw), F32),
            pltpu.VMEM((lt, GDN_V + scw), BF16),
            pltpu.VMEM((lt, OPS_WIDTH), BF16), pltpu.VMEM((lt, OPS_WIDTH), BF16),
            pltpu.VMEM((lt, GDN_QK), F32), pltpu.VMEM((lt, GDN_QK), F32),
            pltpu.VMEM((lt, d), BF16), pltpu.VMEM((lt, d), BF16),
            pltpu.VMEM((lt, 2 * LANES), F32), pltpu.VMEM((lt, 2 * LANES), F32),
            pltpu.VMEM((LANES, lt), F32), pltpu.VMEM((LANES, lt), F32),
        ],
        compiler_params=_params(("arbitrary",)),
        name="mixer_prompt",
    )(x, x, g, w_in, conv_w, sconv_w, alog_v, dtb_v, g_norm, w_out)


def _shift_in_segments(x, buf, s, buf_rows, seg):
    n = x.shape[0]
    rin = _iota2(x.shape, 0) & (seg - 1)
    from_x = pltpu.roll(x, s, axis=0)
    back = (n + s - buf_rows) % n
    from_buf = pltpu.roll(buf, back, axis=0) if back else buf
    return jnp.where(rin >= s, from_x, from_buf)


def _mixer_sample_kernel(*refs, seq, chained):
    it = iter(refs)
    next(it)
    if chained:
        next(it)
    (x_ref, g_ref, win_ref, st_in_ref, qbuf_ref, cbuf_ref, cw_ref, scw_ref, alog_ref, dtb_ref,
     gn_ref, wout_ref, o_ref, st_ref, up_ref, ch_ref, mix_scr) = it
    n = x_ref.shape[0]
    nseq = n // seq
    lseq = seq.bit_length() - 1
    heads = range(GDN_HEADS)
    scw = cbuf_ref.shape[1]

    x = x_ref[...]
    u_pre, z, b_gate, c_gate, h_gate, gate_blk = _in_proj(x, g_ref, win_ref, scw)
    qbuf = qbuf_ref[...]
    conv = cw_ref[GDN_CONV - 1:GDN_CONV, :] * u_pre
    for s in range(1, GDN_CONV):
        conv = conv + cw_ref[GDN_CONV - 1 - s:GDN_CONV - s, :] * _shift_in_segments(
            u_pre, qbuf, s, GDN_CONV - 1, seq)
    qkv = _silu(conv)

    u, w, qk, qd, kt, eglast, tiles = _heads_prep(qkv, gate_blk, alog_ref[...], dtb_ref[...], seq)

    rseq = _iota2((n, GDN_DV), 0) >> lseq
    kt_t = [kt[h].T for h in heads]
    v_parts = [[] for _ in heads]
    o_parts = [[] for _ in heads]
    for s in range(nseq):
        rs = slice(s * seq, (s + 1) * seq)
        ws = [_dot(jnp.concatenate([w[h][rs], qd[h][rs]], axis=0), st_in_ref[s, h]) for h in heads]
        for h in heads:
            v_parts[h].append(u[h][rs] - ws[h][:seq])
            o_parts[h].append(ws[h][seq:])
    v_new = [jnp.concatenate(v_parts[h], axis=0) for h in heads]
    for h in heads:
        hs = slice(h * GDN_DV, (h + 1) * GDN_DV)
        o = jnp.concatenate(o_parts[h], axis=0) + jnp.concatenate(
            [_dot(qk[h][j], v_new[h][rt]) for j, rt in enumerate(tiles)], axis=0)
        mix_scr[:, hs] = _gated_out_norm(o, z[:, hs], gn_ref[...]).astype(BF16)
    for s in range(nseq):
        for h in heads:
            v_s = jnp.where(rseq == s, v_new[h], 0.0)
            st_ref[s, h] = (st_in_ref[s, h]
                            * eglast[s * seq:s * seq + 1, GATE_A_LANE + h:GATE_A_LANE + h + 1]
                            + _dot(kt_t[h], v_s))

    ch = c_gate * h_gate
    cbuf = cbuf_ref[...]
    y = scw_ref[SC_CONV - 1:SC_CONV, :] * ch
    for s in range(1, SC_CONV):
        y = y + scw_ref[SC_CONV - 1 - s:SC_CONV - s, :] * _shift_in_segments(ch, cbuf, s, SC_CONV - 1, seq)
    mix_scr[:, GDN_V:GDN_V + scw] = (b_gate * y).astype(BF16)
    o_ref[...] = x + jnp.dot(mix_scr[...], wout_ref[...], preferred_element_type=F32)
    up_ref[...] = u_pre
    ch_ref[...] = ch


def _mixer_sample(x_new, x, g, w_in, state, st_stack, qbuf, cbuf, conv_w, sconv_w, alog_v, dtb_v, g_norm,
                  w_out, layer, *, row0, n_seq, seq):
    n_all, d = x.shape
    depth = state.shape[0]
    assert seq == SUBLANES
    rows = _pick_tile(n_seq * seq, 128)
    spb = rows // seq
    assert n_seq % spb == 0 and row0 % rows == 0
    blk0 = row0 // rows
    cols = w_in.shape[2]
    scw = sconv_w.shape[2]
    chained = st_stack is not None
    kern = functools.partial(_mixer_sample_kernel, seq=seq, chained=chained)
    lead = [x_new] + ([st_stack] if chained else [])
    aliases = {0: 0, 1: 1} if chained else {0: 0}
    return pl.pallas_call(
        kern,
        grid=(n_seq // spb,),
        in_specs=[pl.BlockSpec(memory_space=pl.ANY)] * len(lead) + [
            pl.BlockSpec((rows, d), lambda i: (blk0 + i, 0)),
            pl.BlockSpec((None, 1, d), lambda i: (layer, 0, 0)),
            pl.BlockSpec((None, d, cols), lambda i: (layer, 0, 0)),
            pl.BlockSpec((None, spb, GDN_HEADS, GDN_DK, GDN_DV), lambda i: (layer, i, 0, 0, 0)),
            pl.BlockSpec((rows, QKV_WIDTH), lambda i: (i, 0)),
            pl.BlockSpec((rows, scw), lambda i: (i, 0)),
            pl.BlockSpec((None, GDN_CONV, QKV_WIDTH), lambda i: (layer, 0, 0)),
            pl.BlockSpec((None, SC_CONV, scw), lambda i: (layer, 0, 0)),
            pl.BlockSpec((None, 1, LANES), lambda i: (layer, 0, 0)),
            pl.BlockSpec((None, 1, LANES), lambda i: (layer, 0, 0)),
            pl.BlockSpec((None, 1, GDN_DV), lambda i: (layer, 0, 0)),
            pl.BlockSpec((None, GDN_V + scw, d), lambda i: (layer, 0, 0)),
        ],
        out_specs=[
            pl.BlockSpec((rows, d), lambda i: (blk0 + i, 0)),
            pl.BlockSpec((None, spb, GDN_HEADS, GDN_DK, GDN_DV), lambda i: (layer, i, 0, 0, 0)),
            pl.BlockSpec((rows, QKV_WIDTH), lambda i: (i, 0)),
            pl.BlockSpec((rows, scw), lambda i: (i, 0)),
        ],
        out_shape=[
            jax.ShapeDtypeStruct((n_all, d), F32),
            jax.ShapeDtypeStruct((depth, n_seq, GDN_HEADS, GDN_DK, GDN_DV), F32),
            jax.ShapeDtypeStruct((n_seq * seq, QKV_WIDTH), F32),
            jax.ShapeDtypeStruct((n_seq * seq, scw), F32),
        ],
        scratch_shapes=[pltpu.VMEM((rows, GDN_V + scw), BF16)],
        input_output_aliases=aliases,
        compiler_params=_params(("arbitrary",)),
        name="mixer_sample",
    )(*lead, x, g, w_in, state, qbuf, cbuf, conv_w, sconv_w, alog_v, dtb_v, g_norm, w_out)


def _softmax_rows(s):
    m = jnp.max(s, axis=-1, keepdims=True)
    p = jnp.exp(s - m)
    return p / jnp.sum(p, axis=-1, keepdims=True)


def _mem_kv_kernel(*refs, chained):
    it = iter(refs)
    if chained:
        next(it)
        next(it)
    mem_ref, wk_ref, wv_ref, kb_ref, vb_ref, k5_ref, v5_ref = it
    nh, hd = k5_ref.shape[1:]
    mb = mem_ref[...].astype(BF16)
    for w_ref, b_ref, o5_ref in ((wk_ref, kb_ref, k5_ref), (wv_ref, vb_ref, v5_ref)):
        acc = jnp.dot(mb, w_ref[...], preferred_element_type=F32)
        b_ref[...] = acc.astype(BF16)
        for h in range(nh):
            o5_ref[:, h, :] = acc[:, h * hd:(h + 1) * hd]


def _mem_kv(mem2, w_k, w_v, k_stack, v_stack, layer, *, depth, batch, mem_len):
    d = mem2.shape[1]
    hd = d // X_HEADS
    chained = k_stack is not None
    lead = [k_stack, v_stack] if chained else []
    kern = functools.partial(_mem_kv_kernel, chained=chained)
    stack_shape = jax.ShapeDtypeStruct((depth, batch, mem_len, X_HEADS, hd), F32)
    spec5 = pl.BlockSpec((None, None, mem_len, X_HEADS, hd), lambda b: (layer, b, 0, 0, 0))
    return pl.pallas_call(
        kern,
        grid=(batch,),
        in_specs=[pl.BlockSpec(memory_space=pl.ANY)] * len(lead) + [
            pl.BlockSpec((mem_len, d), lambda b: (b, 0)),
            pl.BlockSpec((None, d, d), lambda b: (layer, 0, 0)),
            pl.BlockSpec((None, d, d), lambda b: (layer, 0, 0)),
        ],
        out_specs=[
            pl.BlockSpec((mem_len, d), lambda b: (b, 0)),
            pl.BlockSpec((mem_len, d), lambda b: (b, 0)),
            spec5, spec5,
        ],
        out_shape=[
            jax.ShapeDtypeStruct((batch * mem_len, d), BF16),
            jax.ShapeDtypeStruct((batch * mem_len, d), BF16),
            stack_shape, stack_shape,
        ],
        input_output_aliases={0: 2, 1: 3} if chained else {},
        compiler_params=_params(("arbitrary",)),
        name="mem_kv",
    )(*lead, mem2, w_k, w_v)


def _xattn_prompt_kernel(x_ref, g_ref, wq_ref, k_ref, v_ref, wo_ref, o_ref, att_scr):
    hd = x_ref.shape[1] // X_HEADS
    x = x_ref[...]
    xb = _rms(x, g_ref[...]).astype(BF16)
    heads = [slice(h * hd, (h + 1) * hd) for h in range(X_HEADS)]
    q = [jnp.dot(xb, wq_ref[:, hs], preferred_element_type=F32).astype(BF16) for hs in heads]
    s = [_dot_nt(q[h], k_ref[:, hs]) * (hd ** -0.5) for h, hs in enumerate(heads)]
    p = [_softmax_rows(sh).astype(BF16) for sh in s]
    for h, hs in enumerate(heads):
        att_scr[:, hs] = _dot(p[h], v_ref[:, hs]).astype(BF16)
    o_ref[...] = x + jnp.dot(att_scr[...], wo_ref[...], preferred_element_type=F32)


def _xattn_prompt(x, g, w_q, mk, mv, w_o, layer, *, batch, seq, mem_len):
    n_all, d = x.shape
    tq = _pick_tile(seq, ROW_TILE)
    nq = seq // tq
    return pl.pallas_call(
        _xattn_prompt_kernel,
        grid=(batch, nq),
        in_specs=[
            pl.BlockSpec((tq, d), lambda b, t: (b * nq + t, 0)),
            pl.BlockSpec((None, 1, d), lambda b, t: (layer, 0, 0)),
            pl.BlockSpec((None, d, d), lambda b, t: (layer, 0, 0)),
            pl.BlockSpec((mem_len, d), lambda b, t: (b, 0)),
            pl.BlockSpec((mem_len, d), lambda b, t: (b, 0)),
            pl.BlockSpec((None, d, d), lambda b, t: (layer, 0, 0)),
        ],
        out_specs=pl.BlockSpec((tq, d), lambda b, t: (b * nq + t, 0)),
        out_shape=jax.ShapeDtypeStruct((n_all, d), F32),
        scratch_shapes=[pltpu.VMEM((tq, d), BF16)],
        compiler_params=_params(("arbitrary", "arbitrary")),
        name="xattn_prompt",
    )(x, g, w_q, mk, mv, w_o)


def _attn_sample_kernel(q_ref, k_ref, v_ref, o_ref, *, seq, spb):
    mem_len, nh, hd = k_ref.shape[1:]
    lseq = seq.bit_length() - 1
    rows_h = _iota2((nh * seq, mem_len * nh), 0) >> lseq
    cols_h = _iota2((nh * seq, mem_len * nh), 1) & (nh - 1)
    own_head = rows_h == cols_h
    for s in range(spb):
        rs = slice(s * seq, (s + 1) * seq)
        q = q_ref[rs, :]
        q_all = jnp.concatenate([q[:, h * hd:(h + 1) * hd] for h in range(nh)], axis=0)
        k_all = k_ref[s].reshape(mem_len * nh, hd)
        v_all = v_ref[s].reshape(mem_len * nh, hd)
        sc = jnp.where(own_head, _dot_nt(q_all, k_all) * (hd ** -0.5), -jnp.inf)
        o_all = _dot(_softmax_rows(sc), v_all)
        for h in range(nh):
            o_ref[rs, h * hd:(h + 1) * hd] = o_all[h * seq:(h + 1) * seq].astype(o_ref.dtype)


def _attn_sample(q, cache_k, cache_v, layer, *, n_seq, seq):
    d = q.shape[1]
    _, _, mem_len, nh, hd = cache_k.shape
    assert nh & (nh - 1) == 0
    spb = 8
    assert n_seq % spb == 0
    rows = spb * seq
    kern = functools.partial(_attn_sample_kernel, seq=seq, spb=spb)
    return pl.pallas_call(
        kern,
        grid=(n_seq // spb,),
        in_specs=[
            pl.BlockSpec((rows, d), lambda i: (i, 0)),
            pl.BlockSpec((None, spb, mem_len, nh, hd), lambda i: (layer, i, 0, 0, 0)),
            pl.BlockSpec((None, spb, mem_len, nh, hd), lambda i: (layer, i, 0, 0, 0)),
        ],
        out_specs=pl.BlockSpec((rows, d), lambda i: (i, 0)),
        out_shape=jax.ShapeDtypeStruct(q.shape, BF16),
        compiler_params=_params(("arbitrary",)),
        name="attn_sample",
    )(q, cache_k, cache_v)


def _gate_lane_vector(v, lane0):
    depth, n = v.shape
    out = jnp.zeros((depth, 1, LANES), F32)
    return out.at[:, 0, lane0:lane0 + n].set(v.astype(F32))


def kernel(x_prompt, x_sample, mem_prompt, state_gdn, state_qkv_conv, state_short_conv, cache_mem_k, cache_mem_v, g_ffn1, w_ffn1_gu, w_ffn1_down, g_mix, w_in, conv_qkv_w, a_log, dt_bias, g_gdn_out, sconv_w, w_out, g_xattn, w_xq, w_xk, w_xv, w_xo, g_ffn2, w_ffn2_gu, w_ffn2_down, g_final):
    batch, seq, d = x_prompt.shape
    dec_batch, dec_seq, _ = x_sample.shape
    depth = w_in.shape[0]
    mem_len = mem_prompt.shape[1]
    scw = sconv_w.shape[2]
    n_p = batch * seq
    n_s = dec_batch * dec_seq
    off_beta = QKV_WIDTH + GDN_V
    off_sc = off_beta + 2 * GDN_HEADS

    w_in_p = jnp.concatenate(
        [w_in[:, :, :off_beta], w_in[:, :, off_sc:], w_in[:, :, off_beta:off_sc],
         jnp.zeros((depth, d, LANES - 2 * GDN_HEADS), w_in.dtype)], axis=2).astype(BF16)
    wgu1, wd1 = w_ffn1_gu.astype(BF16), w_ffn1_down.astype(BF16)
    wgu2, wd2 = w_ffn2_gu.astype(BF16), w_ffn2_down.astype(BF16)
    w_out_b, w_xq_b, w_xo_b = w_out.astype(BF16), w_xq.astype(BF16), w_xo.astype(BF16)
    w_xk_b, w_xv_b = w_xk.astype(BF16), w_xv.astype(BF16)
    alog_v = _gate_lane_vector(a_log, GATE_A_LANE)
    dtb_v = _gate_lane_vector(dt_bias, GATE_A_LANE)
    g1, gm, gx, g2 = (t.reshape(depth, 1, d) for t in (g_ffn1, g_mix, g_xattn, g_ffn2))
    gn = g_gdn_out.reshape(depth, 1, GDN_DV)
    gf = g_final.reshape(1, d)
    mem2 = mem_prompt.reshape(batch * mem_len, d)
    qbuf_all = jnp.pad(state_qkv_conv, ((0, 0), (0, 0), (0, dec_seq - (GDN_CONV - 1)), (0, 0)))
    cbuf_all = jnp.pad(state_short_conv, ((0, 0), (0, 0), (0, dec_seq - (SC_CONV - 1)), (0, 0)))

    p_s, p_qb, p_sb, s_qb, s_sb = [], [], [], [], []
    st_stack = k_stack = v_stack = None
    x = None
    for l in range(depth):
        xs = [x_prompt.reshape(n_p, d), x_sample.reshape(n_s, d)] if l == 0 else [x]
        x = _ffn(xs, g1, wgu1, wd1, gf, l, final_norm=False)

        x1, st_p, qb_p, chb_p = _mixer_pipe(x, gm, w_in_p, conv_qkv_w, sconv_w, alog_v, dtb_v, gn,
                                              w_out_b, l, batch=batch, seq=seq)
        x1, st_stack, up_s, ch_s = _mixer_sample(
            x1, x, gm, w_in_p, state_gdn, st_stack, qbuf_all[l].reshape(n_s, QKV_WIDTH),
            cbuf_all[l].reshape(n_s, scw), conv_qkv_w, sconv_w, alog_v, dtb_v, gn, w_out_b, l,
            row0=n_p, n_seq=dec_batch, seq=dec_seq)

        mk, mv, k_stack, v_stack = _mem_kv(mem2, w_xk_b, w_xv_b, k_stack, v_stack, l,
                                           depth=depth, batch=batch, mem_len=mem_len)
        x2 = _xattn_prompt(x1, gx, w_xq_b, mk, mv, w_xo_b, l, batch=batch, seq=seq, mem_len=mem_len)
        q_s = _matmul(x1, w_xq_b, l, n_rows=n_s, x_row0=n_p, g=gx, out_dtype=BF16, name="xq_sample")
        att_s = _attn_sample(q_s, cache_mem_k, cache_mem_v, l, n_seq=dec_batch, seq=dec_seq)
        x2 = _matmul(att_s, w_xo_b, l, n_rows=n_s, res=x1, res_row0=n_p, into=x2, out_row0=n_p,
                     name="xo_sample")

        if l == depth - 1:
            y_prompt, y_sample = _ffn([x2], g2, wgu2, wd2, gf, l, final_norm=True, split_rows=n_p)
        else:
            x = _ffn([x2], g2, wgu2, wd2, gf, l, final_norm=False)

        p_s.append(st_p)
        p_qb.append(qb_p[:, SUBLANES - (GDN_CONV - 1):, :])
        p_sb.append(chb_p[:, SUBLANES - (SC_CONV - 1):, :])
        s_qb.append(up_s.reshape(dec_batch, dec_seq, QKV_WIDTH)[:, dec_seq - (GDN_CONV - 1):, :])
        s_sb.append(ch_s.reshape(dec_batch, dec_seq, scw)[:, dec_seq - (SC_CONV - 1):, :])

    return (y_prompt.reshape(batch, seq, d), y_sample.reshape(dec_batch, dec_seq, d),
            jnp.stack(p_s), jnp.stack(p_qb), jnp.stack(p_sb), k_stack, v_stack,
            st_stack, jnp.stack(s_qb), jnp.stack(s_sb))
```

```python
import functools

import jax
import jax.numpy as jnp
from jax import lax
from jax.experimental import pallas as pl
from jax.experimental.pallas import tpu as pltpu

F32 = jnp.float32
BF16 = jnp.bfloat16

GDN_HEADS = 4
GDN_DK = 128
GDN_DV = 128
GDN_QK = GDN_HEADS * GDN_DK
GDN_V = GDN_HEADS * GDN_DV
QKV_WIDTH = 2 * GDN_QK + GDN_V
GDN_CONV = 4
SC_CONV = 3
CHUNK = 64
MAT_TILE = 128
STAGE_ROWS = 512
X_HEADS = 4
RMS_EPS = 1e-6
LANES = 128
SUBLANES = 8
VMEM_LIMIT = 56 * 1024 * 1024
ROW_TILE = 512
FFN_ROW_TILE = 1024

COL_Z = QKV_WIDTH
COL_SCB = COL_Z + GDN_V
GATE_BETA_LANE = 0
GATE_A_LANE = GDN_HEADS


def _params(sem):
    return pltpu.CompilerParams(dimension_semantics=sem, vmem_limit_bytes=VMEM_LIMIT)


def _pick_tile(n, pref):
    t = min(pref, n)
    while n % t:
        t -= SUBLANES
    assert t > 0 and t % SUBLANES == 0
    return t


def _silu(x):
    return x * jax.nn.sigmoid(x)


def _rms(x, g):
    return x * lax.rsqrt(jnp.mean(x * x, axis=-1, keepdims=True) + RMS_EPS) * g


def _dot(a, b):
    return jnp.dot(a.astype(BF16), b.astype(BF16), preferred_element_type=F32)


def _dot_nt(a, b):
    return lax.dot_general(a.astype(BF16), b.astype(BF16), (((1,), (1,)), ((), ())),
                           preferred_element_type=F32)


def _iota2(shape, dim):
    return lax.broadcasted_iota(jnp.int32, shape, dim)


def _ffn_kernel(*refs, d_ff, col_chunk, final_norm, n_in, n_out, split_tile):
    x_refs = refs[:n_in]
    g_ref, wgu_ref, wd_ref, gf_ref = refs[n_in:n_in + 4]
    o_refs = refs[n_in + 4:n_in + 4 + n_out]
    h_ref = refs[n_in + 4 + n_out]
    i = pl.program_id(0)
    if n_in == 2:
        x = jnp.where(i < split_tile, x_refs[0][...], x_refs[1][...])
    else:
        x = x_refs[0][...]
    xb = _rms(x, g_ref[...]).astype(BF16)
    for c in range(d_ff // col_chunk):
        lo = c * col_chunk
        gate = jnp.dot(xb, wgu_ref[:, lo:lo + col_chunk], preferred_element_type=F32)
        up = jnp.dot(xb, wgu_ref[:, d_ff + lo:d_ff + lo + col_chunk], preferred_element_type=F32)
        h_ref[:, lo:lo + col_chunk] = (_silu(gate) * up).astype(BF16)
    y = x + 0.5 * jnp.dot(h_ref[...], wd_ref[...], preferred_element_type=F32)
    if final_norm:
        y = _rms(y, gf_ref[...])
    if n_out == 2:
        @pl.when(i < split_tile)
        def _():
            o_refs[0][...] = y

        @pl.when(i >= split_tile)
        def _():
            o_refs[1][...] = y
    else:
        o_refs[0][...] = y


def _ffn(xs, g, wgu, wd, g_final, layer, *, final_norm, split_rows=None):
    d = xs[0].shape[1]
    n = sum(a.shape[0] for a in xs)
    d_ff = wd.shape[1]
    tm = FFN_ROW_TILE
    col_chunk = 256
    assert d_ff % col_chunk == 0 and all(a.shape[0] % tm == 0 for a in xs)
    if len(xs) == 2:
        split_tile = xs[0].shape[0] // tm
        x_specs = [pl.BlockSpec((tm, d), lambda i: (jnp.minimum(i, split_tile - 1), 0)),
                   pl.BlockSpec((tm, d), lambda i: (jnp.maximum(i - split_tile, 0), 0))]
    else:
        split_tile = None
        x_specs = [pl.BlockSpec((tm, d), lambda i: (i, 0))]
    if split_rows is not None:
        assert split_rows % tm == 0
        split_tile = split_rows // tm
        out_specs = [pl.BlockSpec((tm, d), lambda i: (jnp.minimum(i, split_tile - 1), 0)),
                     pl.BlockSpec((tm, d), lambda i: (jnp.maximum(i - split_tile, 0), 0))]
        out_shape = [jax.ShapeDtypeStruct((split_rows, d), F32),
                     jax.ShapeDtypeStruct((n - split_rows, d), F32)]
    else:
        out_specs = [pl.BlockSpec((tm, d), lambda i: (i, 0))]
        out_shape = [jax.ShapeDtypeStruct((n, d), F32)]
    kern = functools.partial(_ffn_kernel, d_ff=d_ff, col_chunk=col_chunk, final_norm=final_norm,
                             n_in=len(xs), n_out=len(out_specs), split_tile=split_tile)
    out = pl.pallas_call(
        kern,
        grid=(n // tm,),
        in_specs=x_specs + [
            pl.BlockSpec((None, 1, d), lambda i: (layer, 0, 0)),
            pl.BlockSpec((None, d, 2 * d_ff), lambda i: (layer, 0, 0), pipeline_mode=pl.Buffered(1)),
            pl.BlockSpec((None, d_ff, d), lambda i: (layer, 0, 0), pipeline_mode=pl.Buffered(1)),
            pl.BlockSpec((1, d), lambda i: (0, 0)),
        ],
        out_specs=out_specs,
        out_shape=out_shape,
        scratch_shapes=[pltpu.VMEM((tm, d_ff), BF16)],
        compiler_params=_params(("arbitrary",)),
        name="ffn",
    )(*xs, g, wgu, wd, g_final)
    return out if split_rows is not None else out[0]


def _mm_kernel(*refs, rms, residual, aliased):
    it = iter(refs)
    if aliased:
        next(it)
    x_ref = next(it)
    g_ref = next(it) if rms else None
    w_ref = next(it)
    r_ref = next(it) if residual else None
    o_ref = next(it)
    x = x_ref[...]
    if rms:
        x = _rms(x, g_ref[...])
    acc = jnp.dot(x.astype(BF16), w_ref[...], preferred_element_type=F32)
    if residual:
        acc = acc + r_ref[...]
    o_ref[...] = acc.astype(o_ref.dtype)


def _matmul(x, w, layer, *, n_rows, x_row0=0, g=None, res=None, res_row0=0, into=None, out_row0=0,
            out_dtype=F32, name="mm"):
    k = x.shape[1]
    n_out = w.shape[2]
    tm = _pick_tile(n_rows, ROW_TILE)
    assert x_row0 % tm == 0 and res_row0 % tm == 0 and out_row0 % tm == 0
    xb0, rb0, ob0 = x_row0 // tm, res_row0 // tm, out_row0 // tm
    args, specs = [], []
    if into is not None:
        args.append(into)
        specs.append(pl.BlockSpec(memory_space=pl.ANY))
    args.append(x)
    specs.append(pl.BlockSpec((tm, k), lambda i: (xb0 + i, 0)))
    if g is not None:
        args.append(g)
        specs.append(pl.BlockSpec((None, 1, k), lambda i: (layer, 0, 0)))
    args.append(w)
    specs.append(pl.BlockSpec((None, k, n_out), lambda i: (layer, 0, 0)))
    if res is not None:
        args.append(res)
        specs.append(pl.BlockSpec((tm, n_out), lambda i: (rb0 + i, 0)))
    kern = functools.partial(_mm_kernel, rms=g is not None, residual=res is not None,
                             aliased=into is not None)
    out_shape = (jax.ShapeDtypeStruct(into.shape, into.dtype) if into is not None
                 else jax.ShapeDtypeStruct((n_rows, n_out), out_dtype))
    return pl.pallas_call(
        kern,
        grid=(n_rows // tm,),
        in_specs=specs,
        out_specs=pl.BlockSpec((tm, n_out), lambda i: (ob0 + i, 0)),
        out_shape=out_shape,
        input_output_aliases={0: 0} if into is not None else {},
        compiler_params=_params(("arbitrary",)),
        name=name,
    )(*args)


_DONE = object()


def _co_schedule(main, filler):
    for n in main:
        for _ in range(n or 0):
            next(filler, _DONE)
    for _ in filler:
        pass


def _run(stream):
    while True:
        try:
            next(stream)
        except StopIteration as stop:
            return stop.value


def _unit_lower_inverse_steps(a_list, merge_levels, fill=None):
    n = a_list[0].shape[0]
    r = _iota2((n, n), 0)
    c = _iota2((n, n), 1)
    eye = (r == c).astype(F32)
    blk8 = (r >> 3) == (c >> 3)
    d = [jnp.where(blk8, a, 0.0) for a in a_list]
    d2 = [_dot(x, x) for x in d]
    yield fill
    p = [_dot(eye - x, eye + y) for x, y in zip(d, d2)]
    d4 = [_dot(y, y) for y in d2]
    yield fill
    t = [_dot(x, eye + y) for x, y in zip(p, d4)]
    yield fill
    for lvl in merge_levels:
        mask = ((r >> (lvl + 1)) == (c >> (lvl + 1))) & ((r >> lvl) != (c >> lvl))
        lt = [_dot(jnp.where(mask, a, 0.0), x) for a, x in zip(a_list, t)]
        yield fill
        t = [x - _dot(x, y) for x, y in zip(t, lt)]
        yield fill
    return t


def _unit_lower_inverse(a_list, merge_levels):
    return _run(_unit_lower_inverse_steps(a_list, merge_levels))


def _l2n(t):
    return t * lax.rsqrt(jnp.sum(t * t, axis=-1, keepdims=True) + 1e-6)


def _segment_cumsum(g, seg):
    rin = _iota2(g.shape, 0) & (seg - 1)
    sh = 1
    while sh < seg:
        g = g + jnp.where(rin >= sh, pltpu.roll(g, sh, axis=0), 0.0)
        sh *= 2
    return g


def _segment_last(gc, seg):
    n = gc.shape[0]
    rin = _iota2(gc.shape, 0) & (seg - 1)
    x = jnp.where(rin == seg - 1, gc, 0.0)
    sh = 1
    while sh < seg:
        x = x + jnp.where(rin + sh <= seg - 1, pltpu.roll(x, n - sh, axis=0), 0.0)
        sh *= 2
    return x


def _gates(gate_blk, alog, dtb, seg):
    beta = jax.nn.sigmoid(gate_blk)
    xg = gate_blk + dtb
    softplus = jnp.maximum(xg, 0.0) + jnp.log1p(jnp.exp(-jnp.abs(xg)))
    g = -jnp.exp(alog) * softplus
    gc = _segment_cumsum(g, seg)
    return beta, gc


def _heads_prep(qkv, gate_blk, alog, dtb, seg):
    n = qkv.shape[0]
    mt = min(n, MAT_TILE)
    assert n % mt == 0 and mt % seg == 0
    tiles = [slice(j * mt, (j + 1) * mt) for j in range(n // mt)]
    lg = seg.bit_length() - 1
    beta, gc = _gates(gate_blk, alog, dtb, seg)
    egc = jnp.exp(gc)
    glast = _segment_last(gc, seg)
    ekt = jnp.exp(glast - gc)
    eglast = jnp.exp(glast)
    gc_t = gc.T
    r = _iota2((mt, mt), 0)
    c = _iota2((mt, mt), 1)
    same = (r >> lg) == (c >> lg)
    tri = (r >= c) & same
    strict = (r > c) & same
    heads = range(GDN_HEADS)
    q = [_l2n(qkv[:, h * GDN_DK:(h + 1) * GDN_DK]) * (GDN_DK ** -0.5) for h in heads]
    k = [_l2n(qkv[:, GDN_QK + h * GDN_DK:GDN_QK + (h + 1) * GDN_DK]) for h in heads]
    v = [qkv[:, 2 * GDN_QK + h * GDN_DV:2 * GDN_QK + (h + 1) * GDN_DV] for h in heads]
    bh = [beta[:, GATE_BETA_LANE + h:GATE_BETA_LANE + h + 1] for h in heads]
    eg = [egc[:, GATE_A_LANE + h:GATE_A_LANE + h + 1] for h in heads]
    kb = [k[h] * bh[h] for h in heads]
    rhs = [jnp.concatenate([v[h] * bh[h], kb[h] * eg[h]], axis=1) for h in heads]
    dec = [[None] * len(tiles) for _ in heads]
    for h in heads:
        al = GATE_A_LANE + h
        for j, rt in enumerate(tiles):
            diff = gc[rt, al:al + 1] - gc_t[al:al + 1, rt]
            dec[h][j] = jnp.where(tri, jnp.exp(jnp.where(tri, diff, 0.0)), 0.0)
    systems = [(h, j) for h in heads for j in range(len(tiles))]
    a = [jnp.where(strict, _dot_nt(kb[h][tiles[j]], k[h][tiles[j]]) * dec[h][j], 0.0)
         for h, j in systems]
    tinv = _unit_lower_inverse(a, merge_levels=tuple(range(3, lg)))
    uw = [_dot(t, rhs[h][tiles[j]]) for t, (h, j) in zip(tinv, systems)]
    uw = [jnp.concatenate(uw[h * len(tiles):(h + 1) * len(tiles)], axis=0) for h in heads]
    u = [x[:, :GDN_DV] for x in uw]
    w = [x[:, GDN_DV:] for x in uw]
    qk = [[_dot_nt(q[h][rt], k[h][rt]) * dec[h][j] for j, rt in enumerate(tiles)] for h in heads]
    qd = [q[h] * eg[h] for h in heads]
    kt = [k[h] * ekt[:, GATE_A_LANE + h:GATE_A_LANE + h + 1] for h in heads]
    return u, w, qk, qd, kt, eglast, tiles


def _gated_out_norm(o, z, gn):
    return _rms(o, gn) * _silu(z)


def _in_proj_cols(x, g_ref, win_ref):
    xb = _rms(x, g_ref[...]).astype(BF16)

    def cols(lo, width):
        return jnp.dot(xb, win_ref[:, lo:lo + width], preferred_element_type=F32)

    return cols


def _in_proj(x, g_ref, win_ref, scw):
    cols = _in_proj_cols(x, g_ref, win_ref)
    return (cols(0, QKV_WIDTH), cols(COL_Z, GDN_V), cols(COL_SCB, scw), cols(COL_SCB + scw, scw),
            cols(COL_SCB + 2 * scw, scw), cols(COL_SCB + 3 * scw, LANES))


OPS_Q, OPS_K, OPS_KB, OPS_QD, OPS_RHS = 0, GDN_QK, 2 * GDN_QK, 3 * GDN_QK, 4 * GDN_QK
OPS_WIDTH = 4 * GDN_QK + GDN_HEADS * 2 * GDN_DV
FILL_PER_CHUNK = 1
PROJ_SLAB = 256


def _mixer_pipe_kernel(xa_ref, xb_ref, g_ref, win_ref, cw_ref, scw_ref, alog_ref, dtb_ref, gn_ref, wout_ref,
                       o_ref, st_ref, qb_ref, chb_ref,
                       s_scr, extq, extc, mix_scr, ops_new, ops_old, kt_new, kt_old, xn_new, xn_old,
                       gate_new, gate_old, gct_new, gct_old, *, lt, nt, n_tiles):
    i = pl.program_id(0)
    ta = jnp.minimum(i, n_tiles - 1) % nt
    tb = jnp.maximum(i - 1, 0) % nt
    heads = range(GDN_HEADS)
    scw = extc.shape[1]
    mt = min(lt, MAT_TILE)
    tiles = [slice(j * mt, (j + 1) * mt) for j in range(lt // mt)]
    lg = CHUNK.bit_length() - 1

    @pl.when(i == 0)
    def _():
        for ref in (ops_old, kt_old, xn_old, gate_old, gct_old):
            ref[...] = jnp.zeros_like(ref)

    @pl.when(ta == 0)
    def _():
        extq[0:SUBLANES, :] = jnp.zeros((SUBLANES, QKV_WIDTH), F32)

    @pl.when(tb == 0)
    def _():
        s_scr[...] = jnp.zeros_like(s_scr)
        extc[0:SUBLANES, :] = jnp.zeros((SUBLANES, scw), F32)

    late = {}
    pending = [("b", COL_SCB), ("c", COL_SCB + scw), ("h", COL_SCB + 2 * scw)]
    ch_tail = []

    def cols_b(lo, width):
        return jnp.dot(xn_old[...], win_ref[:, lo:lo + width], preferred_element_type=F32)

    def stage_a():
        xn = _rms(xa_ref[...], g_ref[...]).astype(BF16)
        xn_new[...] = xn
        gate_blk = jnp.dot(xn, win_ref[:, COL_SCB + 3 * scw:COL_SCB + 3 * scw + LANES],
                           preferred_element_type=F32)
        beta, gc = _gates(gate_blk, alog_ref[...], dtb_ref[...], CHUNK)
        egc = jnp.exp(gc)
        glast = _segment_last(gc, CHUNK)
        ekt = jnp.exp(glast - gc)
        gate_new[:, 0:LANES] = gc
        gate_new[:, LANES:2 * LANES] = jnp.exp(glast)
        gct_new[...] = gc.T
        yield
        act = []
        for slab in range(QKV_WIDTH // PROJ_SLAB):
            sl = slice(slab * PROJ_SLAB, (slab + 1) * PROJ_SLAB)
            extq[SUBLANES:SUBLANES + lt, sl] = jnp.dot(xn, win_ref[:, sl], preferred_element_type=F32)
            for gcol in range(slab * PROJ_SLAB // LANES, (slab + 1) * PROJ_SLAB // LANES):
                cs = slice(gcol * LANES, (gcol + 1) * LANES)
                conv = cw_ref[GDN_CONV - 1:GDN_CONV, cs] * extq[SUBLANES:SUBLANES + lt, cs]
                for s in range(1, GDN_CONV):
                    conv = conv + cw_ref[GDN_CONV - 1 - s:GDN_CONV - s, cs] * extq[SUBLANES - s:SUBLANES - s + lt, cs]
                act.append(_silu(conv))
            yield
        for h in heads:
            hs = slice(h * GDN_DK, (h + 1) * GDN_DK)
            bh = beta[:, GATE_BETA_LANE + h:GATE_BETA_LANE + h + 1]
            eg = egc[:, GATE_A_LANE + h:GATE_A_LANE + h + 1]
            q = _l2n(act[h]) * (GDN_DK ** -0.5)
            k = _l2n(act[GDN_HEADS + h])
            kb = k * bh
            ops_new[:, OPS_Q + h * GDN_DK:OPS_Q + (h + 1) * GDN_DK] = q.astype(BF16)
            ops_new[:, OPS_K + h * GDN_DK:OPS_K + (h + 1) * GDN_DK] = k.astype(BF16)
            ops_new[:, OPS_KB + h * GDN_DK:OPS_KB + (h + 1) * GDN_DK] = kb.astype(BF16)
            ops_new[:, OPS_QD + h * GDN_DK:OPS_QD + (h + 1) * GDN_DK] = (q * eg).astype(BF16)
            lo = OPS_RHS + h * 2 * GDN_DV
            ops_new[:, lo:lo + GDN_DV] = (act[2 * GDN_HEADS + h] * bh).astype(BF16)
            ops_new[:, lo + GDN_DV:lo + 2 * GDN_DV] = (kb * eg).astype(BF16)
            kt_new[:, hs] = k * ekt[:, GATE_A_LANE + h:GATE_A_LANE + h + 1]
            yield

    def stage_b():
        yield 1
        r = _iota2((mt, mt), 0)
        c = _iota2((mt, mt), 1)
        same = (r >> lg) == (c >> lg)
        tri = (r >= c) & same
        strict = (r > c) & same
        systems = [(h, j) for h in heads for j in range(len(tiles))]

        def op(lo, h, rows, width=GDN_DK):
            return ops_old[rows, lo + h * width:lo + (h + 1) * width]

        dec = []
        for h, j in systems:
            al = GATE_A_LANE + h
            diff = gate_old[tiles[j], al:al + 1] - gct_old[al:al + 1, tiles[j]]
            dec.append(jnp.where(tri, jnp.exp(jnp.where(tri, diff, 0.0)), 0.0))
        yield
        a = [jnp.where(strict, _dot_nt(op(OPS_KB, h, tiles[j]), op(OPS_K, h, tiles[j])) * dk, 0.0)
             for (h, j), dk in zip(systems, dec)]
        yield
        tinv = yield from _unit_lower_inverse_steps(a, merge_levels=tuple(range(3, lg)), fill=1)
        uw = [_dot(tk, op(OPS_RHS, h, tiles[j], 2 * GDN_DV)) for (h, j), tk in zip(systems, tinv)]
        uw = [jnp.concatenate(uw[h * len(tiles):(h + 1) * len(tiles)], axis=0) for h in heads]
        yield 1

        def short_conv_group(gcol):
            cs = slice(gcol * LANES, (gcol + 1) * LANES)
            ch = late["c"][:, cs] * late["h"][:, cs]
            extc[SUBLANES:SUBLANES + lt, cs] = ch
            y = scw_ref[SC_CONV - 1:SC_CONV, cs] * ch
            for s in range(1, SC_CONV):
                y = y + scw_ref[SC_CONV - 1 - s:SC_CONV - s, cs] * extc[SUBLANES - s:SUBLANES - s + lt, cs]
            mix_scr[:, GDN_V + gcol * LANES:GDN_V + (gcol + 1) * LANES] = (late["b"][:, cs] * y).astype(BF16)
            ch_tail.append(ch[lt - SUBLANES:lt, :])

        sc_groups = list(range(scw // LANES))
        n_chunks = lt // CHUNK
        qk = []
        qk_per_chunk = -(-len(systems) // n_chunks)
        s_cur = [s_scr[h] for h in heads]
        v_parts = [[] for _ in heads]
        o_parts = [[] for _ in heads]
        for cc in range(n_chunks):
            row0 = cc * CHUNK
            rs = slice(row0, row0 + CHUNK)
            ws = [_dot(jnp.concatenate([uw[h][rs, GDN_DV:], op(OPS_QD, h, rs)], axis=0), s_cur[h])
                  for h in heads]
            vn = [uw[h][rs, :GDN_DV] - ws[h][:CHUNK] for h in heads]
            s_cur = [s_cur[h] * gate_old[row0:row0 + 1, LANES + GATE_A_LANE + h:LANES + GATE_A_LANE + h + 1]
                     + _dot(kt_old[rs, h * GDN_DK:(h + 1) * GDN_DK].T, vn[h]) for h in heads]
            for h in heads:
                v_parts[h].append(vn[h])
                o_parts[h].append(ws[h][CHUNK:])
            if "z" not in late:
                late["z"] = cols_b(COL_Z, GDN_V)
            elif pending:
                name, lo = pending.pop(0)
                late[name] = cols_b(lo, scw)
            elif sc_groups:
                short_conv_group(sc_groups.pop(0))
            for (h, j), dk in list(zip(systems, dec))[len(qk):len(qk) + qk_per_chunk]:
                qk.append(_dot_nt(op(OPS_Q, h, tiles[j]), op(OPS_K, h, tiles[j])) * dk)
            yield FILL_PER_CHUNK
        while pending:
            name, lo = pending.pop(0)
            late[name] = cols_b(lo, scw)
        while sc_groups:
            short_conv_group(sc_groups.pop(0))
        for h in heads:
            s_scr[h] = s_cur[h]
            v_new = jnp.concatenate(v_parts[h], axis=0)
            o = jnp.concatenate(o_parts[h], axis=0) + jnp.concatenate(
                [_dot(qk[h * len(tiles) + j], v_new[rt]) for j, rt in enumerate(tiles)], axis=0)
            hs = slice(h * GDN_DV, (h + 1) * GDN_DV)
            mix_scr[:, hs] = _gated_out_norm(o, late["z"][:, hs], gn_ref[...]).astype(BF16)
            yield

    _co_schedule(stage_b(), stage_a())

    o_ref[...] = xb_ref[...] + jnp.dot(mix_scr[...], wout_ref[...], preferred_element_type=F32)

    u_tail = extq[lt:lt + SUBLANES, :]
    c_tail = jnp.concatenate(ch_tail, axis=1)
    extq[0:SUBLANES, :] = u_tail
    extc[0:SUBLANES, :] = c_tail
    for new, old in ((ops_new, ops_old), (kt_new, kt_old), (xn_new, xn_old), (gate_new, gate_old),
                     (gct_new, gct_old)):
        old[...] = new[...]

    @pl.when(ta == nt - 1)
    def _():
        qb_ref[...] = u_tail

    @pl.when(tb == nt - 1)
    def _():
        st_ref[...] = s_scr[...]
        chb_ref[...] = c_tail


def _mixer_pipe(x, g, w_in, conv_w, sconv_w, alog_v, dtb_v, g_norm, w_out, layer, *, batch, seq):
    n_all, d = x.shape
    lt = _pick_tile(seq, STAGE_ROWS)
    assert lt % CHUNK == 0
    nt = seq // lt
    n_tiles = batch * nt
    cols = w_in.shape[2]
    scw = sconv_w.shape[2]
    kern = functools.partial(_mixer_pipe_kernel, lt=lt, nt=nt, n_tiles=n_tiles)

    def tile_a(i):
        return jnp.minimum(i, n_tiles - 1)

    def tile_b(i):
        return jnp.maximum(i - 1, 0)

    return pl.pallas_call(
        kern,
        grid=(n_tiles + 1,),
        in_specs=[
            pl.BlockSpec((lt, d), lambda i: (tile_a(i), 0)),
            pl.BlockSpec((lt, d), lambda i: (tile_b(i), 0)),
            pl.BlockSpec((None, 1, d), lambda i: (layer, 0, 0)),
            pl.BlockSpec((None, d, cols), lambda i: (layer, 0, 0)),
            pl.BlockSpec((None, GDN_CONV, QKV_WIDTH), lambda i: (layer, 0, 0)),
            pl.BlockSpec((None, SC_CONV, scw), lambda i: (layer, 0, 0)),
            pl.BlockSpec((None, 1, LANES), lambda i: (layer, 0, 0)),
            pl.BlockSpec((None, 1, LANES), lambda i: (layer, 0, 0)),
            pl.BlockSpec((None, 1, GDN_DV), lambda i: (layer, 0, 0)),
            pl.BlockSpec((None, GDN_V + scw, d), lambda i: (layer, 0, 0)),
        ],
        out_specs=[
            pl.BlockSpec((lt, d), lambda i: (tile_b(i), 0)),
            pl.BlockSpec((None, GDN_HEADS, GDN_DK, GDN_DV), lambda i: (tile_b(i) // nt, 0, 0, 0)),
            pl.BlockSpec((None, SUBLANES, QKV_WIDTH), lambda i: (tile_a(i) // nt, 0, 0)),
            pl.BlockSpec((None, SUBLANES, scw), lambda i: (tile_b(i) // nt, 0, 0)),
        ],
        out_shape=[
            jax.ShapeDtypeStruct((n_all, d), F32),
            jax.ShapeDtypeStruct((batch, GDN_HEADS, GDN_DK, GDN_DV), F32),
            jax.ShapeDtypeStruct((batch, SUBLANES, QKV_WIDTH), F32),
            jax.ShapeDtypeStruct((batch, SUBLANES, scw), F32),
        ],
        scratch_shapes=[
            pltpu.VMEM((GDN_HEADS, GDN_DK, GDN_DV), F32),
            pltpu.VMEM((lt + SUBLANES, QKV_WIDTH), F32),
            pltpu.VMEM((lt + SUBLANES, scw), F32),
            pltpu.VMEM((lt, GDN_V + scw), BF16),
            pltpu.VMEM((lt, OPS_WIDTH), BF16), pltpu.VMEM((lt, OPS_WIDTH), BF16),
            pltpu.VMEM((lt, GDN_QK), F32), pltpu.VMEM((lt, GDN_QK), F32),
            pltpu.VMEM((lt, d), BF16), pltpu.VMEM((lt, d), BF16),
            pltpu.VMEM((lt, 2 * LANES), F32), pltpu.VMEM((lt, 2 * LANES), F32),
            pltpu.VMEM((LANES, lt), F32), pltpu.VMEM((LANES, lt), F32),
        ],
        compiler_params=_params(("arbitrary",)),
        name="mixer_prompt",
    )(x, x, g, w_in, conv_w, sconv_w, alog_v, dtb_v, g_norm, w_out)


def _shift_in_segments(x, buf, s, buf_rows, seg):
    n = x.shape[0]
    rin = _iota2(x.shape, 0) & (seg - 1)
    from_x = pltpu.roll(x, s, axis=0)
    back = (n + s - buf_rows) % n
    from_buf = pltpu.roll(buf, back, axis=0) if back else buf
    return jnp.where(rin >= s, from_x, from_buf)


def _mixer_sample_kernel(*refs, seq, chained):
    it = iter(refs)
    next(it)
    if chained:
        next(it)
    (x_ref, g_ref, win_ref, st_in_ref, qbuf_ref, cbuf_ref, cw_ref, scw_ref, alog_ref, dtb_ref,
     gn_ref, wout_ref, o_ref, st_ref, up_ref, ch_ref, mix_scr) = it
    n = x_ref.shape[0]
    nseq = n // seq
    lseq = seq.bit_length() - 1
    heads = range(GDN_HEADS)
    scw = cbuf_ref.shape[1]

    x = x_ref[...]
    u_pre, z, b_gate, c_gate, h_gate, gate_blk = _in_proj(x, g_ref, win_ref, scw)
    qbuf = qbuf_ref[...]
    conv = cw_ref[GDN_CONV - 1:GDN_CONV, :] * u_pre
    for s in range(1, GDN_CONV):
        conv = conv + cw_ref[GDN_CONV - 1 - s:GDN_CONV - s, :] * _shift_in_segments(
            u_pre, qbuf, s, GDN_CONV - 1, seq)
    qkv = _silu(conv)

    u, w, qk, qd, kt, eglast, tiles = _heads_prep(qkv, gate_blk, alog_ref[...], dtb_ref[...], seq)

    rseq = _iota2((n, GDN_DV), 0) >> lseq
    kt_t = [kt[h].T for h in heads]
    v_parts = [[] for _ in heads]
    o_parts = [[] for _ in heads]
    for s in range(nseq):
        rs = slice(s * seq, (s + 1) * seq)
        ws = [_dot(jnp.concatenate([w[h][rs], qd[h][rs]], axis=0), st_in_ref[s, h]) for h in heads]
        for h in heads:
            v_parts[h].append(u[h][rs] - ws[h][:seq])
            o_parts[h].append(ws[h][seq:])
    v_new = [jnp.concatenate(v_parts[h], axis=0) for h in heads]
    for h in heads:
        hs = slice(h * GDN_DV, (h + 1) * GDN_DV)
        o = jnp.concatenate(o_parts[h], axis=0) + jnp.concatenate(
            [_dot(qk[h][j], v_new[h][rt]) for j, rt in enumerate(tiles)], axis=0)
        mix_scr[:, hs] = _gated_out_norm(o, z[:, hs], gn_ref[...]).astype(BF16)
    for s in range(nseq):
        for h in heads:
            v_s = jnp.where(rseq == s, v_new[h], 0.0)
            st_ref[s, h] = (st_in_ref[s, h]
                            * eglast[s * seq:s * seq + 1, GATE_A_LANE + h:GATE_A_LANE + h + 1]
                            + _dot(kt_t[h], v_s))

    ch = c_gate * h_gate
    cbuf = cbuf_ref[...]
    y = scw_ref[SC_CONV - 1:SC_CONV, :] * ch
    for s in range(1, SC_CONV):
        y = y + scw_ref[SC_CONV - 1 - s:SC_CONV - s, :] * _shift_in_segments(ch, cbuf, s, SC_CONV - 1, seq)
    mix_scr[:, GDN_V:GDN_V + scw] = (b_gate * y).astype(BF16)
    o_ref[...] = x + jnp.dot(mix_scr[...], wout_ref[...], preferred_element_type=F32)
    up_ref[...] = u_pre
    ch_ref[...] = ch


def _mixer_sample(x_new, x, g, w_in, state, st_stack, qbuf, cbuf, conv_w, sconv_w, alog_v, dtb_v, g_norm,
                  w_out, layer, *, row0, n_seq, seq):
    n_all, d = x.shape
    depth = state.shape[0]
    assert seq == SUBLANES
    rows = _pick_tile(n_seq * seq, 128)
    spb = rows // seq
    assert n_seq % spb == 0 and row0 % rows == 0
    blk0 = row0 // rows
    cols = w_in.shape[2]
    scw = sconv_w.shape[2]
    chained = st_stack is not None
    kern = functools.partial(_mixer_sample_kernel, seq=seq, chained=chained)
    lead = [x_new] + ([st_stack] if chained else [])
    aliases = {0: 0, 1: 1} if chained else {0: 0}
    return pl.pallas_call(
        kern,
        grid=(n_seq // spb,),
        in_specs=[pl.BlockSpec(memory_space=pl.ANY)] * len(lead) + [
            pl.BlockSpec((rows, d), lambda i: (blk0 + i, 0)),
            pl.BlockSpec((None, 1, d), lambda i: (layer, 0, 0)),
            pl.BlockSpec((None, d, cols), lambda i: (layer, 0, 0)),
            pl.BlockSpec((None, spb, GDN_HEADS, GDN_DK, GDN_DV), lambda i: (layer, i, 0, 0, 0)),
            pl.BlockSpec((rows, QKV_WIDTH), lambda i: (i, 0)),
            pl.BlockSpec((rows, scw), lambda i: (i, 0)),
            pl.BlockSpec((None, GDN_CONV, QKV_WIDTH), lambda i: (layer, 0, 0)),
            pl.BlockSpec((None, SC_CONV, scw), lambda i: (layer, 0, 0)),
            pl.BlockSpec((None, 1, LANES), lambda i: (layer, 0, 0)),
            pl.BlockSpec((None, 1, LANES), lambda i: (layer, 0, 0)),
            pl.BlockSpec((None, 1, GDN_DV), lambda i: (layer, 0, 0)),
            pl.BlockSpec((None, GDN_V + scw, d), lambda i: (layer, 0, 0)),
        ],
        out_specs=[
            pl.BlockSpec((rows, d), lambda i: (blk0 + i, 0)),
            pl.BlockSpec((None, spb, GDN_HEADS, GDN_DK, GDN_DV), lambda i: (layer, i, 0, 0, 0)),
            pl.BlockSpec((rows, QKV_WIDTH), lambda i: (i, 0)),
            pl.BlockSpec((rows, scw), lambda i: (i, 0)),
        ],
        out_shape=[
            jax.ShapeDtypeStruct((n_all, d), F32),
            jax.ShapeDtypeStruct((depth, n_seq, GDN_HEADS, GDN_DK, GDN_DV), F32),
            jax.ShapeDtypeStruct((n_seq * seq, QKV_WIDTH), F32),
            jax.ShapeDtypeStruct((n_seq * seq, scw), F32),
        ],
        scratch_shapes=[pltpu.VMEM((rows, GDN_V + scw), BF16)],
        input_output_aliases=aliases,
        compiler_params=_params(("arbitrary",)),
        name="mixer_sample",
    )(*lead, x, g, w_in, state, qbuf, cbuf, conv_w, sconv_w, alog_v, dtb_v, g_norm, w_out)


def _softmax_rows(s):
    m = jnp.max(s, axis=-1, keepdims=True)
    p = jnp.exp(s - m)
    return p / jnp.sum(p, axis=-1, keepdims=True)


def _mem_kv_kernel(*refs, chained):
    it = iter(refs)
    if chained:
        next(it)
        next(it)
    mem_ref, wk_ref, wv_ref, kb_ref, vb_ref, k5_ref, v5_ref = it
    nh, hd = k5_ref.shape[1:]
    mb = mem_ref[...].astype(BF16)
    for w_ref, b_ref, o5_ref in ((wk_ref, kb_ref, k5_ref), (wv_ref, vb_ref, v5_ref)):
        acc = jnp.dot(mb, w_ref[...], preferred_element_type=F32)
        b_ref[...] = acc.astype(BF16)
        for h in range(nh):
            o5_ref[:, h, :] = acc[:, h * hd:(h + 1) * hd]


def _mem_kv(mem2, w_k, w_v, k_stack, v_stack, layer, *, depth, batch, mem_len):
    d = mem2.shape[1]
    hd = d // X_HEADS
    chained = k_stack is not None
    lead = [k_stack, v_stack] if chained else []
    kern = functools.partial(_mem_kv_kernel, chained=chained)
    stack_shape = jax.ShapeDtypeStruct((depth, batch, mem_len, X_HEADS, hd), F32)
    spec5 = pl.BlockSpec((None, None, mem_len, X_HEADS, hd), lambda b: (layer, b, 0, 0, 0))
    return pl.pallas_call(
        kern,
        grid=(batch,),
        in_specs=[pl.BlockSpec(memory_space=pl.ANY)] * len(lead) + [
            pl.BlockSpec((mem_len, d), lambda b: (b, 0)),
            pl.BlockSpec((None, d, d), lambda b: (layer, 0, 0)),
            pl.BlockSpec((None, d, d), lambda b: (layer, 0, 0)),
        ],
        out_specs=[
            pl.BlockSpec((mem_len, d), lambda b: (b, 0)),
            pl.BlockSpec((mem_len, d), lambda b: (b, 0)),
            spec5, spec5,
        ],
        out_shape=[
            jax.ShapeDtypeStruct((batch * mem_len, d), BF16),
            jax.ShapeDtypeStruct((batch * mem_len, d), BF16),
            stack_shape, stack_shape,
        ],
        input_output_aliases={0: 2, 1: 3} if chained else {},
        compiler_params=_params(("arbitrary",)),
        name="mem_kv",
    )(*lead, mem2, w_k, w_v)


def _xattn_prompt_kernel(x_ref, g_ref, wq_ref, k_ref, v_ref, wo_ref, o_ref, att_scr):
    hd = x_ref.shape[1] // X_HEADS
    x = x_ref[...]
    xb = _rms(x, g_ref[...]).astype(BF16)
    heads = [slice(h * hd, (h + 1) * hd) for h in range(X_HEADS)]
    q = [jnp.dot(xb, wq_ref[:, hs], preferred_element_type=F32).astype(BF16) for hs in heads]
    s = [_dot_nt(q[h], k_ref[:, hs]) * (hd ** -0.5) for h, hs in enumerate(heads)]
    p = [_softmax_rows(sh).astype(BF16) for sh in s]
    for h, hs in enumerate(heads):
        att_scr[:, hs] = _dot(p[h], v_ref[:, hs]).astype(BF16)
    o_ref[...] = x + jnp.dot(att_scr[...], wo_ref[...], preferred_element_type=F32)


def _xattn_prompt(x, g, w_q, mk, mv, w_o, layer, *, batch, seq, mem_len):
    n_all, d = x.shape
    tq = _pick_tile(seq, ROW_TILE)
    nq = seq // tq
    return pl.pallas_call(
        _xattn_prompt_kernel,
        grid=(batch, nq),
        in_specs=[
            pl.BlockSpec((tq, d), lambda b, t: (b * nq + t, 0)),
            pl.BlockSpec((None, 1, d), lambda b, t: (layer, 0, 0)),
            pl.BlockSpec((None, d, d), lambda b, t: (layer, 0, 0)),
            pl.BlockSpec((mem_len, d), lambda b, t: (b, 0)),
            pl.BlockSpec((mem_len, d), lambda b, t: (b, 0)),
            pl.BlockSpec((None, d, d), lambda b, t: (layer, 0, 0)),
        ],
        out_specs=pl.BlockSpec((tq, d), lambda b, t: (b * nq + t, 0)),
        out_shape=jax.ShapeDtypeStruct((n_all, d), F32),
        scratch_shapes=[pltpu.VMEM((tq, d), BF16)],
        compiler_params=_params(("arbitrary", "arbitrary")),
        name="xattn_prompt",
    )(x, g, w_q, mk, mv, w_o)


def _attn_sample_kernel(q_ref, k_ref, v_ref, o_ref, *, seq, spb):
    mem_len, nh, hd = k_ref.shape[1:]
    lseq = seq.bit_length() - 1
    rows_h = _iota2((nh * seq, mem_len * nh), 0) >> lseq
    cols_h = _iota2((nh * seq, mem_len * nh), 1) & (nh - 1)
    own_head = rows_h == cols_h
    for s in range(spb):
        rs = slice(s * seq, (s + 1) * seq)
        q = q_ref[rs, :]
        q_all = jnp.concatenate([q[:, h * hd:(h + 1) * hd] for h in range(nh)], axis=0)
        k_all = k_ref[s].reshape(mem_len * nh, hd)
        v_all = v_ref[s].reshape(mem_len * nh, hd)
        sc = jnp.where(own_head, _dot_nt(q_all, k_all) * (hd ** -0.5), -jnp.inf)
        o_all = _dot(_softmax_rows(sc), v_all)
        for h in range(nh):
            o_ref[rs, h * hd:(h + 1) * hd] = o_all[h * seq:(h + 1) * seq].astype(o_ref.dtype)


def _attn_sample(q, cache_k, cache_v, layer, *, n_seq, seq):
    d = q.shape[1]
    _, _, mem_len, nh, hd = cache_k.shape
    assert nh & (nh - 1) == 0
    spb = 8
    assert n_seq % spb == 0
    rows = spb * seq
    kern = functools.partial(_attn_sample_kernel, seq=seq, spb=spb)
    return pl.pallas_call(
        kern,
        grid=(n_seq // spb,),
        in_specs=[
            pl.BlockSpec((rows, d), lambda i: (i, 0)),
            pl.BlockSpec((None, spb, mem_len, nh, hd), lambda i: (layer, i, 0, 0, 0)),
            pl.BlockSpec((None, spb, mem_len, nh, hd), lambda i: (layer, i, 0, 0, 0)),
        ],
        out_specs=pl.BlockSpec((rows, d), lambda i: (i, 0)),
        out_shape=jax.ShapeDtypeStruct(q.shape, BF16),
        compiler_params=_params(("arbitrary",)),
        name="attn_sample",
    )(q, cache_k, cache_v)


def _gate_lane_vector(v, lane0):
    depth, n = v.shape
    out = jnp.zeros((depth, 1, LANES), F32)
    return out.at[:, 0, lane0:lane0 + n].set(v.astype(F32))


def kernel(x_prompt, x_sample, mem_prompt, state_gdn, state_qkv_conv, state_short_conv, cache_mem_k, cache_mem_v, g_ffn1, w_ffn1_gu, w_ffn1_down, g_mix, w_in, conv_qkv_w, a_log, dt_bias, g_gdn_out, sconv_w, w_out, g_xattn, w_xq, w_xk, w_xv, w_xo, g_ffn2, w_ffn2_gu, w_ffn2_down, g_final):
    batch, seq, d = x_prompt.shape
    dec_batch, dec_seq, _ = x_sample.shape
    depth = w_in.shape[0]
    mem_len = mem_prompt.shape[1]
    scw = sconv_w.shape[2]
    n_p = batch * seq
    n_s = dec_batch * dec_seq
    off_beta = QKV_WIDTH + GDN_V
    off_sc = off_beta + 2 * GDN_HEADS

    w_in_p = jnp.concatenate(
        [w_in[:, :, :off_beta], w_in[:, :, off_sc:], w_in[:, :, off_beta:off_sc],
         jnp.zeros((depth, d, LANES - 2 * GDN_HEADS), w_in.dtype)], axis=2).astype(BF16)
    wgu1, wd1 = w_ffn1_gu.astype(BF16), w_ffn1_down.astype(BF16)
    wgu2, wd2 = w_ffn2_gu.astype(BF16), w_ffn2_down.astype(BF16)
    w_out_b, w_xq_b, w_xo_b = w_out.astype(BF16), w_xq.astype(BF16), w_xo.astype(BF16)
    w_xk_b, w_xv_b = w_xk.astype(BF16), w_xv.astype(BF16)
    alog_v = _gate_lane_vector(a_log, GATE_A_LANE)
    dtb_v = _gate_lane_vector(dt_bias, GATE_A_LANE)
    g1, gm, gx, g2 = (t.reshape(depth, 1, d) for t in (g_ffn1, g_mix, g_xattn, g_ffn2))
    gn = g_gdn_out.reshape(depth, 1, GDN_DV)
    gf = g_final.reshape(1, d)
    mem2 = mem_prompt.reshape(batch * mem_len, d)
    qbuf_all = jnp.pad(state_qkv_conv, ((0, 0), (0, 0), (0, dec_seq - (GDN_CONV - 1)), (0, 0)))
    cbuf_all = jnp.pad(state_short_conv, ((0, 0), (0, 0), (0, dec_seq - (SC_CONV - 1)), (0, 0)))

    p_s, p_qb, p_sb, s_qb, s_sb = [], [], [], [], []
    st_stack = k_stack = v_stack = None
    x = None
    for l in range(depth):
        xs = [x_prompt.reshape(n_p, d), x_sample.reshape(n_s, d)] if l == 0 else [x]
        x = _ffn(xs, g1, wgu1, wd1, gf, l, final_norm=False)

        x1, st_p, qb_p, chb_p = _mixer_pipe(x, gm, w_in_p, conv_qkv_w, sconv_w, alog_v, dtb_v, gn,
                                              w_out_b, l, batch=batch, seq=seq)
        x1, st_stack, up_s, ch_s = _mixer_sample(
            x1, x, gm, w_in_p, state_gdn, st_stack, qbuf_all[l].reshape(n_s, QKV_WIDTH),
            cbuf_all[l].reshape(n_s, scw), conv_qkv_w, sconv_w, alog_v, dtb_v, gn, w_out_b, l,
            row0=n_p, n_seq=dec_batch, seq=dec_seq)

        mk, mv, k_stack, v_stack = _mem_kv(mem2, w_xk_b, w_xv_b, k_stack, v_stack, l,
                                           depth=depth, batch=batch, mem_len=mem_len)
        x2 = _xattn_prompt(x1, gx, w_xq_b, mk, mv, w_xo_b, l, batch=batch, seq=seq, mem_len=mem_len)
        q_s = _matmul(x1, w_xq_b, l, n_rows=n_s, x_row0=n_p, g=gx, out_dtype=BF16, name="xq_sample")
        att_s = _attn_sample(q_s, cache_mem_k, cache_mem_v, l, n_seq=dec_batch, seq=dec_seq)
        x2 = _matmul(att_s, w_xo_b, l, n_rows=n_s, res=x1, res_row0=n_p, into=x2, out_row0=n_p,
                     name="xo_sample")

        if l == depth - 1:
            y_prompt, y_sample = _ffn([x2], g2, wgu2, wd2, gf, l, final_norm=True, split_rows=n_p)
        else:
            x = _ffn([x2], g2, wgu2, wd2, gf, l, final_norm=False)

        p_s.append(st_p)
        p_qb.append(qb_p[:, SUBLANES - (GDN_CONV - 1):, :])
        p_sb.append(chb_p[:, SUBLANES - (SC_CONV - 1):, :])
        s_qb.append(up_s.reshape(dec_batch, dec_seq, QKV_WIDTH)[:, dec_seq - (GDN_CONV - 1):, :])
        s_sb.append(ch_s.reshape(dec_batch, dec_seq, scw)[:, dec_seq - (SC_CONV - 1):, :])

    return (y_prompt.reshape(batch, seq, d), y_sample.reshape(dec_batch, dec_seq, d),
            jnp.stack(p_s), jnp.stack(p_qb), jnp.stack(p_sb), k_stack, v_stack,
            st_stack, jnp.stack(s_qb), jnp.stack(s_sb))
```

```python
import functools

import jax
import jax.numpy as jnp
from jax import lax
from jax.experimental import pallas as pl
from jax.experimental.pallas import tpu as pltpu

F32 = jnp.float32
BF16 = jnp.bfloat16

GDN_HEADS = 4
GDN_DK = 128
GDN_DV = 128
GDN_QK = GDN_HEADS * GDN_DK
GDN_V = GDN_HEADS * GDN_DV
QKV_WIDTH = 2 * GDN_QK + GDN_V
GDN_CONV = 4
SC_CONV = 3
CHUNK = 64
MAT_TILE = 128
STAGE_ROWS = 512
X_HEADS = 4
RMS_EPS = 1e-6
LANES = 128
SUBLANES = 8
VMEM_LIMIT = 56 * 1024 * 1024
ROW_TILE = 512
FFN_ROW_TILE = 1024

COL_Z = QKV_WIDTH
COL_SCB = COL_Z + GDN_V
GATE_BETA_LANE = 0
GATE_A_LANE = GDN_HEADS


def _params(sem):
    return pltpu.CompilerParams(dimension_semantics=sem, vmem_limit_bytes=VMEM_LIMIT)


def _pick_tile(n, pref):
    t = min(pref, n)
    while n % t:
        t -= SUBLANES
    assert t > 0 and t % SUBLANES == 0
    return t


def _silu(x):
    return x * jax.nn.sigmoid(x)


def _rms(x, g):
    return x * lax.rsqrt(jnp.mean(x * x, axis=-1, keepdims=True) + RMS_EPS) * g


def _dot(a, b):
    return jnp.dot(a.astype(BF16), b.astype(BF16), preferred_element_type=F32)


def _dot_nt(a, b):
    return lax.dot_general(a.astype(BF16), b.astype(BF16), (((1,), (1,)), ((), ())),
                           preferred_element_type=F32)


def _iota2(shape, dim):
    return lax.broadcasted_iota(jnp.int32, shape, dim)


def _ffn_kernel(*refs, d_ff, col_chunk, final_norm, n_in, n_out, split_tile):
    x_refs = refs[:n_in]
    g_ref, wgu_ref, wd_ref, gf_ref = refs[n_in:n_in + 4]
    o_refs = refs[n_in + 4:n_in + 4 + n_out]
    h_ref = refs[n_in + 4 + n_out]
    i = pl.program_id(0)
    if n_in == 2:
        x = jnp.where(i < split_tile, x_refs[0][...], x_refs[1][...])
    else:
        x = x_refs[0][...]
    xb = _rms(x, g_ref[...]).astype(BF16)
    for c in range(d_ff // col_chunk):
        lo = c * col_chunk
        gate = jnp.dot(xb, wgu_ref[:, lo:lo + col_chunk], preferred_element_type=F32)
        up = jnp.dot(xb, wgu_ref[:, d_ff + lo:d_ff + lo + col_chunk], preferred_element_type=F32)
        h_ref[:, lo:lo + col_chunk] = (_silu(gate) * up).astype(BF16)
    y = x + 0.5 * jnp.dot(h_ref[...], wd_ref[...], preferred_element_type=F32)
    if final_norm:
        y = _rms(y, gf_ref[...])
    if n_out == 2:
        @pl.when(i < split_tile)
        def _():
            o_refs[0][...] = y

        @pl.when(i >= split_tile)
        def _():
            o_refs[1][...] = y
    else:
        o_refs[0][...] = y


def _ffn(xs, g, wgu, wd, g_final, layer, *, final_norm, split_rows=None):
    d = xs[0].shape[1]
    n = sum(a.shape[0] for a in xs)
    d_ff = wd.shape[1]
    tm = FFN_ROW_TILE
    col_chunk = 256
    assert d_ff % col_chunk == 0 and all(a.shape[0] % tm == 0 for a in xs)
    if len(xs) == 2:
        split_tile = xs[0].shape[0] // tm
        x_specs = [pl.BlockSpec((tm, d), lambda i: (jnp.minimum(i, split_tile - 1), 0)),
                   pl.BlockSpec((tm, d), lambda i: (jnp.maximum(i - split_tile, 0), 0))]
    else:
        split_tile = None
        x_specs = [pl.BlockSpec((tm, d), lambda i: (i, 0))]
    if split_rows is not None:
        assert split_rows % tm == 0
        split_tile = split_rows // tm
        out_specs = [pl.BlockSpec((tm, d), lambda i: (jnp.minimum(i, split_tile - 1), 0)),
                     pl.BlockSpec((tm, d), lambda i: (jnp.maximum(i - split_tile, 0), 0))]
        out_shape = [jax.ShapeDtypeStruct((split_rows, d), F32),
                     jax.ShapeDtypeStruct((n - split_rows, d), F32)]
    else:
        out_specs = [pl.BlockSpec((tm, d), lambda i: (i, 0))]
        out_shape = [jax.ShapeDtypeStruct((n, d), F32)]
    kern = functools.partial(_ffn_kernel, d_ff=d_ff, col_chunk=col_chunk, final_norm=final_norm,
                             n_in=len(xs), n_out=len(out_specs), split_tile=split_tile)
    out = pl.pallas_call(
        kern,
        grid=(n // tm,),
        in_specs=x_specs + [
            pl.BlockSpec((None, 1, d), lambda i: (layer, 0, 0)),
            pl.BlockSpec((None, d, 2 * d_ff), lambda i: (layer, 0, 0), pipeline_mode=pl.Buffered(1)),
            pl.BlockSpec((None, d_ff, d), lambda i: (layer, 0, 0), pipeline_mode=pl.Buffered(1)),
            pl.BlockSpec((1, d), lambda i: (0, 0)),
        ],
        out_specs=out_specs,
        out_shape=out_shape,
        scratch_shapes=[pltpu.VMEM((tm, d_ff), BF16)],
        compiler_params=_params(("arbitrary",)),
        name="ffn",
    )(*xs, g, wgu, wd, g_final)
    return out if split_rows is not None else out[0]


def _mm_kernel(*refs, rms, residual, aliased):
    it = iter(refs)
    if aliased:
        next(it)
    x_ref = next(it)
    g_ref = next(it) if rms else None
    w_ref = next(it)
    r_ref = next(it) if residual else None
    o_ref = next(it)
    x = x_ref[...]
    if rms:
        x = _rms(x, g_ref[...])
    acc = jnp.dot(x.astype(BF16), w_ref[...], preferred_element_type=F32)
    if residual:
        acc = acc + r_ref[...]
    o_ref[...] = acc.astype(o_ref.dtype)


def _matmul(x, w, layer, *, n_rows, x_row0=0, g=None, res=None, res_row0=0, into=None, out_row0=0,
            out_dtype=F32, name="mm"):
    k = x.shape[1]
    n_out = w.shape[2]
    tm = _pick_tile(n_rows, ROW_TILE)
    assert x_row0 % tm == 0 and res_row0 % tm == 0 and out_row0 % tm == 0
    xb0, rb0, ob0 = x_row0 // tm, res_row0 // tm, out_row0 // tm
    args, specs = [], []
    if into is not None:
        args.append(into)
        specs.append(pl.BlockSpec(memory_space=pl.ANY))
    args.append(x)
    specs.append(pl.BlockSpec((tm, k), lambda i: (xb0 + i, 0)))
    if g is not None:
        args.append(g)
        specs.append(pl.BlockSpec((None, 1, k), lambda i: (layer, 0, 0)))
    args.append(w)
    specs.append(pl.BlockSpec((None, k, n_out), lambda i: (layer, 0, 0)))
    if res is not None:
        args.append(res)
        specs.append(pl.BlockSpec((tm, n_out), lambda i: (rb0 + i, 0)))
    kern = functools.partial(_mm_kernel, rms=g is not None, residual=res is not None,
                             aliased=into is not None)
    out_shape = (jax.ShapeDtypeStruct(into.shape, into.dtype) if into is not None
                 else jax.ShapeDtypeStruct((n_rows, n_out), out_dtype))
    return pl.pallas_call(
        kern,
        grid=(n_rows // tm,),
        in_specs=specs,
        out_specs=pl.BlockSpec((tm, n_out), lambda i: (ob0 + i, 0)),
        out_shape=out_shape,
        input_output_aliases={0: 0} if into is not None else {},
        compiler_params=_params(("arbitrary",)),
        name=name,
    )(*args)


_DONE = object()


def _co_schedule(main, filler):
    for n in main:
        for _ in range(n or 0):
            next(filler, _DONE)
    for _ in filler:
        pass


def _run(stream):
    while True:
        try:
            next(stream)
        except StopIteration as stop:
            return stop.value


def _unit_lower_inverse_steps(a_list, merge_levels, fill=None):
    n = a_list[0].shape[0]
    r = _iota2((n, n), 0)
    c = _iota2((n, n), 1)
    eye = (r == c).astype(F32)
    blk8 = (r >> 3) == (c >> 3)
    d = [jnp.where(blk8, a, 0.0) for a in a_list]
    d2 = [_dot(x, x) for x in d]
    yield fill
    p = [_dot(eye - x, eye + y) for x, y in zip(d, d2)]
    d4 = [_dot(y, y) for y in d2]
    yield fill
    t = [_dot(x, eye + y) for x, y in zip(p, d4)]
    yield fill
    for lvl in merge_levels:
        mask = ((r >> (lvl + 1)) == (c >> (lvl + 1))) & ((r >> lvl) != (c >> lvl))
        lt = [_dot(jnp.where(mask, a, 0.0), x) for a, x in zip(a_list, t)]
        yield fill
        t = [x - _dot(x, y) for x, y in zip(t, lt)]
        yield fill
    return t


def _unit_lower_inverse(a_list, merge_levels):
    return _run(_unit_lower_inverse_steps(a_list, merge_levels))


def _l2n(t):
    return t * lax.rsqrt(jnp.sum(t * t, axis=-1, keepdims=True) + 1e-6)


def _segment_cumsum(g, seg):
    rin = _iota2(g.shape, 0) & (seg - 1)
    sh = 1
    while sh < seg:
        g = g + jnp.where(rin >= sh, pltpu.roll(g, sh, axis=0), 0.0)
        sh *= 2
    return g


def _segment_last(gc, seg):
    n = gc.shape[0]
    rin = _iota2(gc.shape, 0) & (seg - 1)
    x = jnp.where(rin == seg - 1, gc, 0.0)
    sh = 1
    while sh < seg:
        x = x + jnp.where(rin + sh <= seg - 1, pltpu.roll(x, n - sh, axis=0), 0.0)
        sh *= 2
    return x


def _gates(gate_blk, alog, dtb, seg):
    beta = jax.nn.sigmoid(gate_blk)
    xg = gate_blk + dtb
    softplus = jnp.maximum(xg, 0.0) + jnp.log1p(jnp.exp(-jnp.abs(xg)))
    g = -jnp.exp(alog) * softplus
    gc = _segment_cumsum(g, seg)
    return beta, gc


def _heads_prep(qkv, gate_blk, alog, dtb, seg):
    n = qkv.shape[0]
    mt = min(n, MAT_TILE)
    assert n % mt == 0 and mt % seg == 0
    tiles = [slice(j * mt, (j + 1) * mt) for j in range(n // mt)]
    lg = seg.bit_length() - 1
    beta, gc = _gates(gate_blk, alog, dtb, seg)
    egc = jnp.exp(gc)
    glast = _segment_last(gc, seg)
    ekt = jnp.exp(glast - gc)
    eglast = jnp.exp(glast)
    gc_t = gc.T
    r = _iota2((mt, mt), 0)
    c = _iota2((mt, mt), 1)
    same = (r >> lg) == (c >> lg)
    tri = (r >= c) & same
    strict = (r > c) & same
    heads = range(GDN_HEADS)
    q = [_l2n(qkv[:, h * GDN_DK:(h + 1) * GDN_DK]) * (GDN_DK ** -0.5) for h in heads]
    k = [_l2n(qkv[:, GDN_QK + h * GDN_DK:GDN_QK + (h + 1) * GDN_DK]) for h in heads]
    v = [qkv[:, 2 * GDN_QK + h * GDN_DV:2 * GDN_QK + (h + 1) * GDN_DV] for h in heads]
    bh = [beta[:, GATE_BETA_LANE + h:GATE_BETA_LANE + h + 1] for h in heads]
    eg = [egc[:, GATE_A_LANE + h:GATE_A_LANE + h + 1] for h in heads]
    kb = [k[h] * bh[h] for h in heads]
    rhs = [jnp.concatenate([v[h] * bh[h], kb[h] * eg[h]], axis=1) for h in heads]
    dec = [[None] * len(tiles) for _ in heads]
    for h in heads:
        al = GATE_A_LANE + h
        for j, rt in enumerate(tiles):
            diff = gc[rt, al:al + 1] - gc_t[al:al + 1, rt]
            dec[h][j] = jnp.where(tri, jnp.exp(jnp.where(tri, diff, 0.0)), 0.0)
    systems = [(h, j) for h in heads for j in range(len(tiles))]
    a = [jnp.where(strict, _dot_nt(kb[h][tiles[j]], k[h][tiles[j]]) * dec[h][j], 0.0)
         for h, j in systems]
    tinv = _unit_lower_inverse(a, merge_levels=tuple(range(3, lg)))
    uw = [_dot(t, rhs[h][tiles[j]]) for t, (h, j) in zip(tinv, systems)]
    uw = [jnp.concatenate(uw[h * len(tiles):(h + 1) * len(tiles)], axis=0) for h in heads]
    u = [x[:, :GDN_DV] for x in uw]
    w = [x[:, GDN_DV:] for x in uw]
    qk = [[_dot_nt(q[h][rt], k[h][rt]) * dec[h][j] for j, rt in enumerate(tiles)] for h in heads]
    qd = [q[h] * eg[h] for h in heads]
    kt = [k[h] * ekt[:, GATE_A_LANE + h:GATE_A_LANE + h + 1] for h in heads]
    return u, w, qk, qd, kt, eglast, tiles


def _gated_out_norm(o, z, gn):
    return _rms(o, gn) * _silu(z)


def _in_proj_cols(x, g_ref, win_ref):
    xb = _rms(x, g_ref[...]).astype(BF16)

    def cols(lo, width):
        return jnp.dot(xb, win_ref[:, lo:lo + width], preferred_element_type=F32)

    return cols


def _in_proj(x, g_ref, win_ref, scw):
    cols = _in_proj_cols(x, g_ref, win_ref)
    return (cols(0, QKV_WIDTH), cols(COL_Z, GDN_V), cols(COL_SCB, scw), cols(COL_SCB + scw, scw),
            cols(COL_SCB + 2 * scw, scw), cols(COL_SCB + 3 * scw, LANES))


OPS_Q, OPS_K, OPS_KB, OPS_QD, OPS_RHS = 0, GDN_QK, 2 * GDN_QK, 3 * GDN_QK, 4 * GDN_QK
OPS_WIDTH = 4 * GDN_QK + GDN_HEADS * 2 * GDN_DV
FILL_PER_CHUNK = 1
PROJ_SLAB = 256


def _mixer_pipe_kernel(xa_ref, xb_ref, g_ref, win_ref, cw_ref, scw_ref, alog_ref, dtb_ref, gn_ref, wout_ref,
                       o_ref, st_ref, qb_ref, chb_ref,
                       s_scr, extq, extc, mix_scr, ops_new, ops_old, kt_new, kt_old, xn_new, xn_old,
                       gate_new, gate_old, gct_new, gct_old, *, lt, nt, n_tiles):
    i = pl.program_id(0)
    ta = jnp.minimum(i, n_tiles - 1) % nt
    tb = jnp.maximum(i - 1, 0) % nt
    heads = range(GDN_HEADS)
    scw = extc.shape[1]
    mt = min(lt, MAT_TILE)
    tiles = [slice(j * mt, (j + 1) * mt) for j in range(lt // mt)]
    lg = CHUNK.bit_length() - 1

    @pl.when(i == 0)
    def _():
        for ref in (ops_old, kt_old, xn_old, gate_old, gct_old):
            ref[...] = jnp.zeros_like(ref)

    @pl.when(ta == 0)
    def _():
        extq[0:SUBLANES, :] = jnp.zeros((SUBLANES, QKV_WIDTH), F32)

    @pl.when(tb == 0)
    def _():
        s_scr[...] = jnp.zeros_like(s_scr)
        extc[0:SUBLANES, :] = jnp.zeros((SUBLANES, scw), F32)

    late = {}
    pending = [("b", COL_SCB), ("c", COL_SCB + scw), ("h", COL_SCB + 2 * scw)]
    ch_tail = []

    def cols_b(lo, width):
        return jnp.dot(xn_old[...], win_ref[:, lo:lo + width], preferred_element_type=F32)

    def stage_a():
        xn = _rms(xa_ref[...], g_ref[...]).astype(BF16)
        xn_new[...] = xn
        gate_blk = jnp.dot(xn, win_ref[:, COL_SCB + 3 * scw:COL_SCB + 3 * scw + LANES],
                           preferred_element_type=F32)
        beta, gc = _gates(gate_blk, alog_ref[...], dtb_ref[...], CHUNK)
        egc = jnp.exp(gc)
        glast = _segment_last(gc, CHUNK)
        ekt = jnp.exp(glast - gc)
        gate_new[:, 0:LANES] = gc
        gate_new[:, LANES:2 * LANES] = jnp.exp(glast)
        gct_new[...] = gc.T
        yield
        act = []
        for slab in range(QKV_WIDTH // PROJ_SLAB):
            sl = slice(slab * PROJ_SLAB, (slab + 1) * PROJ_SLAB)
            extq[SUBLANES:SUBLANES + lt, sl] = jnp.dot(xn, win_ref[:, sl], preferred_element_type=F32)
            for gcol in range(slab * PROJ_SLAB // LANES, (slab + 1) * PROJ_SLAB // LANES):
                cs = slice(gcol * LANES, (gcol + 1) * LANES)
                conv = cw_ref[GDN_CONV - 1:GDN_CONV, cs] * extq[SUBLANES:SUBLANES + lt, cs]
                for s in range(1, GDN_CONV):
                    conv = conv + cw_ref[GDN_CONV - 1 - s:GDN_CONV - s, cs] * extq[SUBLANES - s:SUBLANES - s + lt, cs]
                act.append(_silu(conv))
            yield
        for h in heads:
            hs = slice(h * GDN_DK, (h + 1) * GDN_DK)
            bh = beta[:, GATE_BETA_LANE + h:GATE_BETA_LANE + h + 1]
            eg = egc[:, GATE_A_LANE + h:GATE_A_LANE + h + 1]
            q = _l2n(act[h]) * (GDN_DK ** -0.5)
            k = _l2n(act[GDN_HEADS + h])
            kb = k * bh
            ops_new[:, OPS_Q + h * GDN_DK:OPS_Q + (h + 1) * GDN_DK] = q.astype(BF16)
            ops_new[:, OPS_K + h * GDN_DK:OPS_K + (h + 1) * GDN_DK] = k.astype(BF16)
            ops_new[:, OPS_KB + h * GDN_DK:OPS_KB + (h + 1) * GDN_DK] = kb.astype(BF16)
            ops_new[:, OPS_QD + h * GDN_DK:OPS_QD + (h + 1) * GDN_DK] = (q * eg).astype(BF16)
            lo = OPS_RHS + h * 2 * GDN_DV
            ops_new[:, lo:lo + GDN_DV] = (act[2 * GDN_HEADS + h] * bh).astype(BF16)
            ops_new[:, lo + GDN_DV:lo + 2 * GDN_DV] = (kb * eg).astype(BF16)
            kt_new[:, hs] = k * ekt[:, GATE_A_LANE + h:GATE_A_LANE + h + 1]
            yield

    def stage_b():
        yield 1
        r = _iota2((mt, mt), 0)
        c = _iota2((mt, mt), 1)
        same = (r >> lg) == (c >> lg)
        tri = (r >= c) & same
        strict = (r > c) & same
        systems = [(h, j) for h in heads for j in range(len(tiles))]

        def op(lo, h, rows, width=GDN_DK):
            return ops_old[rows, lo + h * width:lo + (h + 1) * width]

        dec = []
        for h, j in systems:
            al = GATE_A_LANE + h
            diff = gate_old[tiles[j], al:al + 1] - gct_old[al:al + 1, tiles[j]]
            dec.append(jnp.where(tri, jnp.exp(jnp.where(tri, diff, 0.0)), 0.0))
        yield
        a = [jnp.where(strict, _dot_nt(op(OPS_KB, h, tiles[j]), op(OPS_K, h, tiles[j])) * dk, 0.0)
             for (h, j), dk in zip(systems, dec)]
        yield
        tinv = yield from _unit_lower_inverse_steps(a, merge_levels=tuple(range(3, lg)), fill=1)
        uw = [_dot(tk, op(OPS_RHS, h, tiles[j], 2 * GDN_DV)) for (h, j), tk in zip(systems, tinv)]
        uw = [jnp.concatenate(uw[h * len(tiles):(h + 1) * len(tiles)], axis=0) for h in heads]
        yield 1

        def short_conv_group(gcol):
            cs = slice(gcol * LANES, (gcol + 1) * LANES)
            ch = late["c"][:, cs] * late["h"][:, cs]
            extc[SUBLANES:SUBLANES + lt, cs] = ch
            y = scw_ref[SC_CONV - 1:SC_CONV, cs] * ch
            for s in range(1, SC_CONV):
                y = y + scw_ref[SC_CONV - 1 - s:SC_CONV - s, cs] * extc[SUBLANES - s:SUBLANES - s + lt, cs]
            mix_scr[:, GDN_V + gcol * LANES:GDN_V + (gcol + 1) * LANES] = (late["b"][:, cs] * y).astype(BF16)
            ch_tail.append(ch[lt - SUBLANES:lt, :])

        sc_groups = list(range(scw // LANES))
        n_chunks = lt // CHUNK
        qk = []
        qk_per_chunk = -(-len(systems) // n_chunks)
        s_cur = [s_scr[h] for h in heads]
        v_parts = [[] for _ in heads]
        o_parts = [[] for _ in heads]
        for cc in range(n_chunks):
            row0 = cc * CHUNK
            rs = slice(row0, row0 + CHUNK)
            ws = [_dot(jnp.concatenate([uw[h][rs, GDN_DV:], op(OPS_QD, h, rs)], axis=0), s_cur[h])
                  for h in heads]
            vn = [uw[h][rs, :GDN_DV] - ws[h][:CHUNK] for h in heads]
            s_cur = [s_cur[h] * gate_old[row0:row0 + 1, LANES + GATE_A_LANE + h:LANES + GATE_A_LANE + h + 1]
                     + _dot(kt_old[rs, h * GDN_DK:(h + 1) * GDN_DK].T, vn[h]) for h in heads]
            for h in heads:
                v_parts[h].append(vn[h])
                o_parts[h].append(ws[h][CHUNK:])
            if "z" not in late:
                late["z"] = cols_b(COL_Z, GDN_V)
            elif pending:
                name, lo = pending.pop(0)
                late[name] = cols_b(lo, scw)
            elif sc_groups:
                short_conv_group(sc_groups.pop(0))
            for (h, j), dk in list(zip(systems, dec))[len(qk):len(qk) + qk_per_chunk]:
                qk.append(_dot_nt(op(OPS_Q, h, tiles[j]), op(OPS_K, h, tiles[j])) * dk)
            yield FILL_PER_CHUNK
        while pending:
            name, lo = pending.pop(0)
            late[name] = cols_b(lo, scw)
        while sc_groups:
            short_conv_group(sc_groups.pop(0))
        for h in heads:
            s_scr[h] = s_cur[h]
            v_new = jnp.concatenate(v_parts[h], axis=0)
            o = jnp.concatenate(o_parts[h], axis=0) + jnp.concatenate(
                [_dot(qk[h * len(tiles) + j], v_new[rt]) for j, rt in enumerate(tiles)], axis=0)
            hs = slice(h * GDN_DV, (h + 1) * GDN_DV)
            mix_scr[:, hs] = _gated_out_norm(o, late["z"][:, hs], gn_ref[...]).astype(BF16)
            yield

    _co_schedule(stage_b(), stage_a())

    o_ref[...] = xb_ref[...] + jnp.dot(mix_scr[...], wout_ref[...], preferred_element_type=F32)

    u_tail = extq[lt:lt + SUBLANES, :]
    c_tail = jnp.concatenate(ch_tail, axis=1)
    extq[0:SUBLANES, :] = u_tail
    extc[0:SUBLANES, :] = c_tail
    for new, old in ((ops_new, ops_old), (kt_new, kt_old), (xn_new, xn_old), (gate_new, gate_old),
                     (gct_new, gct_old)):
        old[...] = new[...]

    @pl.when(ta == nt - 1)
    def _():
        qb_ref[...] = u_tail

    @pl.when(tb == nt - 1)
    def _():
        st_ref[...] = s_scr[...]
        chb_ref[...] = c_tail


def _mixer_pipe(x, g, w_in, conv_w, sconv_w, alog_v, dtb_v, g_norm, w_out, layer, *, batch, seq):
    n_all, d = x.shape
    lt = _pick_tile(seq, STAGE_ROWS)
    assert lt % CHUNK == 0
    nt = seq // lt
    n_tiles = batch * nt
    cols = w_in.shape[2]
    scw = sconv_w.shape[2]
    kern = functools.partial(_mixer_pipe_kernel, lt=lt, nt=nt, n_tiles=n_tiles)

    def tile_a(i):
        return jnp.minimum(i, n_tiles - 1)

    def tile_b(i):
        return jnp.maximum(i - 1, 0)

    return pl.pallas_call(
        kern,
        grid=(n_tiles + 1,),
        in_specs=[
            pl.BlockSpec((lt, d), lambda i: (tile_a(i), 0)),
            pl.BlockSpec((lt, d), lambda i: (tile_b(i), 0)),
            pl.BlockSpec((None, 1, d), lambda i: (layer, 0, 0)),
            pl.BlockSpec((None, d, cols), lambda i: (layer, 0, 0)),
            pl.BlockSpec((None, GDN_CONV, QKV_WIDTH), lambda i: (layer, 0, 0)),
            pl.BlockSpec((None, SC_CONV, scw), lambda i: (layer, 0, 0)),
            pl.BlockSpec((None, 1, LANES), lambda i: (layer, 0, 0)),
            pl.BlockSpec((None, 1, LANES), lambda i: (layer, 0, 0)),
            pl.BlockSpec((None, 1, GDN_DV), lambda i: (layer, 0, 0)),
            pl.BlockSpec((None, GDN_V + scw, d), lambda i: (layer, 0, 0)),
        ],
        out_specs=[
            pl.BlockSpec((lt, d), lambda i: (tile_b(i), 0)),
            pl.BlockSpec((None, GDN_HEADS, GDN_DK, GDN_DV), lambda i: (tile_b(i) // nt, 0, 0, 0)),
            pl.BlockSpec((None, SUBLANES, QKV_WIDTH), lambda i: (tile_a(i) // nt, 0, 0)),
            pl.BlockSpec((None, SUBLANES, scw), lambda i: (tile_b(i) // nt, 0, 0)),
        ],
        out_shape=[
            jax.ShapeDtypeStruct((n_all, d), F32),
            jax.ShapeDtypeStruct((batch, GDN_HEADS, GDN_DK, GDN_DV), F32),
            jax.ShapeDtypeStruct((batch, SUBLANES, QKV_WIDTH), F32),
            jax.ShapeDtypeStruct((batch, SUBLANES, scw), F32),
        ],
        scratch_shapes=[
            pltpu.VMEM((GDN_HEADS, GDN_DK, GDN_DV), F32),
            pltpu.VMEM((lt + SUBLANES, QKV_WIDTH), F32),
            pltpu.VMEM((lt + SUBLANES, scw), F32),
            pltpu.VMEM((lt, GDN_V + scw), BF16),
            pltpu.VMEM((lt, OPS_WIDTH), BF16), pltpu.VMEM((lt, OPS_WIDTH), BF16),
            pltpu.VMEM((lt, GDN_QK), F32), pltpu.VMEM((lt, GDN_QK), F32),
            pltpu.VMEM((lt, d), BF16), pltpu.VMEM((lt, d), BF16),
            pltpu.VMEM((lt, 2 * LANES), F32), pltpu.VMEM((lt, 2 * LANES), F32),
            pltpu.VMEM((LANES, lt), F32), pltpu.VMEM((LANES, lt), F32),
        ],
        compiler_params=_params(("arbitrary",)),
        name="mixer_prompt",
    )(x, x, g, w_in, conv_w, sconv_w, alog_v, dtb_v, g_norm, w_out)


def _shift_in_segments(x, buf, s, buf_rows, seg):
    n = x.shape[0]
    rin = _iota2(x.shape, 0) & (seg - 1)
    from_x = pltpu.roll(x, s, axis=0)
    back = (n + s - buf_rows) % n
    from_buf = pltpu.roll(buf, back, axis=0) if back else buf
    return jnp.where(rin >= s, from_x, from_buf)


def _mixer_sample_kernel(*refs, seq, chained):
    it = iter(refs)
    next(it)
    if chained:
        next(it)
    (x_ref, g_ref, win_ref, st_in_ref, qbuf_ref, cbuf_ref, cw_ref, scw_ref, alog_ref, dtb_ref,
     gn_ref, wout_ref, o_ref, st_ref, up_ref, ch_ref, mix_scr) = it
    n = x_ref.shape[0]
    nseq = n // seq
    lseq = seq.bit_length() - 1
    heads = range(GDN_HEADS)
    scw = cbuf_ref.shape[1]

    x = x_ref[...]
    u_pre, z, b_gate, c_gate, h_gate, gate_blk = _in_proj(x, g_ref, win_ref, scw)
    qbuf = qbuf_ref[...]
    conv = cw_ref[GDN_CONV - 1:GDN_CONV, :] * u_pre
    for s in range(1, GDN_CONV):
        conv = conv + cw_ref[GDN_CONV - 1 - s:GDN_CONV - s, :] * _shift_in_segments(
            u_pre, qbuf, s, GDN_CONV - 1, seq)
    qkv = _silu(conv)

    u, w, qk, qd, kt, eglast, tiles = _heads_prep(qkv, gate_blk, alog_ref[...], dtb_ref[...], seq)

    rseq = _iota2((n, GDN_DV), 0) >> lseq
    kt_t = [kt[h].T for h in heads]
    v_parts = [[] for _ in heads]
    o_parts = [[] for _ in heads]
    for s in range(nseq):
        rs = slice(s * seq, (s + 1) * seq)
        ws = [_dot(jnp.concatenate([w[h][rs], qd[h][rs]], axis=0), st_in_ref[s, h]) for h in heads]
        for h in heads:
            v_parts[h].append(u[h][rs] - ws[h][:seq])
            o_parts[h].append(ws[h][seq:])
    v_new = [jnp.concatenate(v_parts[h], axis=0) for h in heads]
    for h in heads:
        hs = slice(h * GDN_DV, (h + 1) * GDN_DV)
        o = jnp.concatenate(o_parts[h], axis=0) + jnp.concatenate(
            [_dot(qk[h][j], v_new[h][rt]) for j, rt in enumerate(tiles)], axis=0)
        mix_scr[:, hs] = _gated_out_norm(o, z[:, hs], gn_ref[...]).astype(BF16)
    for s in range(nseq):
        for h in heads:
            v_s = jnp.where(rseq == s, v_new[h], 0.0)
            st_ref[s, h] = (st_in_ref[s, h]
                            * eglast[s * seq:s * seq + 1, GATE_A_LANE + h:GATE_A_LANE + h + 1]
                            + _dot(kt_t[h], v_s))

    ch = c_gate * h_gate
    cbuf = cbuf_ref[...]
    y = scw_ref[SC_CONV - 1:SC_CONV, :] * ch
    for s in range(1, SC_CONV):
        y = y + scw_ref[SC_CONV - 1 - s:SC_CONV - s, :] * _shift_in_segments(ch, cbuf, s, SC_CONV - 1, seq)
    mix_scr[:, GDN_V:GDN_V + scw] = (b_gate * y).astype(BF16)
    o_ref[...] = x + jnp.dot(mix_scr[...], wout_ref[...], preferred_element_type=F32)
    up_ref[...] = u_pre
    ch_ref[...] = ch


def _mixer_sample(x_new, x, g, w_in, state, st_stack, qbuf, cbuf, conv_w, sconv_w, alog_v, dtb_v, g_norm,
                  w_out, layer, *, row0, n_seq, seq):
    n_all, d = x.shape
    depth = state.shape[0]
    assert seq == SUBLANES
    rows = _pick_tile(n_seq * seq, 128)
    spb = rows // seq
    assert n_seq % spb == 0 and row0 % rows == 0
    blk0 = row0 // rows
    cols = w_in.shape[2]
    scw = sconv_w.shape[2]
    chained = st_stack is not None
    kern = functools.partial(_mixer_sample_kernel, seq=seq, chained=chained)
    lead = [x_new] + ([st_stack] if chained else [])
    aliases = {0: 0, 1: 1} if chained else {0: 0}
    return pl.pallas_call(
        kern,
        grid=(n_seq // spb,),
        in_specs=[pl.BlockSpec(memory_space=pl.ANY)] * len(lead) + [
            pl.BlockSpec((rows, d), lambda i: (blk0 + i, 0)),
            pl.BlockSpec((None, 1, d), lambda i: (layer, 0, 0)),
            pl.BlockSpec((None, d, cols), lambda i: (layer, 0, 0)),
            pl.BlockSpec((None, spb, GDN_HEADS, GDN_DK, GDN_DV), lambda i: (layer, i, 0, 0, 0)),
            pl.BlockSpec((rows, QKV_WIDTH), lambda i: (i, 0)),
            pl.BlockSpec((rows, scw), lambda i: (i, 0)),
            pl.BlockSpec((None, GDN_CONV, QKV_WIDTH), lambda i: (layer, 0, 0)),
            pl.BlockSpec((None, SC_CONV, scw), lambda i: (layer, 0, 0)),
            pl.BlockSpec((None, 1, LANES), lambda i: (layer, 0, 0)),
            pl.BlockSpec((None, 1, LANES), lambda i: (layer, 0, 0)),
            pl.BlockSpec((None, 1, GDN_DV), lambda i: (layer, 0, 0)),
            pl.BlockSpec((None, GDN_V + scw, d), lambda i: (layer, 0, 0)),
        ],
        out_specs=[
            pl.BlockSpec((rows, d), lambda i: (blk0 + i, 0)),
            pl.BlockSpec((None, spb, GDN_HEADS, GDN_DK, GDN_DV), lambda i: (layer, i, 0, 0, 0)),
            pl.BlockSpec((rows, QKV_WIDTH), lambda i: (i, 0)),
            pl.BlockSpec((rows, scw), lambda i: (i, 0)),
        ],
        out_shape=[
            jax.ShapeDtypeStruct((n_all, d), F32),
            jax.ShapeDtypeStruct((depth, n_seq, GDN_HEADS, GDN_DK, GDN_DV), F32),
            jax.ShapeDtypeStruct((n_seq * seq, QKV_WIDTH), F32),
            jax.ShapeDtypeStruct((n_seq * seq, scw), F32),
        ],
        scratch_shapes=[pltpu.VMEM((rows, GDN_V + scw), BF16)],
        input_output_aliases=aliases,
        compiler_params=_params(("arbitrary",)),
        name="mixer_sample",
    )(*lead, x, g, w_in, state, qbuf, cbuf, conv_w, sconv_w, alog_v, dtb_v, g_norm, w_out)


def _softmax_rows(s):
    m = jnp.max(s, axis=-1, keepdims=True)
    p = jnp.exp(s - m)
    return p / jnp.sum(p, axis=-1, keepdims=True)


def _mem_kv_kernel(*refs, chained):
    it = iter(refs)
    if chained:
        next(it)
        next(it)
    mem_ref, wk_ref, wv_ref, kb_ref, vb_ref, k5_ref, v5_ref = it
    nh, hd = k5_ref.shape[1:]
    mb = mem_ref[...].astype(BF16)
    for w_ref, b_ref, o5_ref in ((wk_ref, kb_ref, k5_ref), (wv_ref, vb_ref, v5_ref)):
        acc = jnp.dot(mb, w_ref[...], preferred_element_type=F32)
        b_ref[...] = acc.astype(BF16)
        for h in range(nh):
            o5_ref[:, h, :] = acc[:, h * hd:(h + 1) * hd]


def _mem_kv(mem2, w_k, w_v, k_stack, v_stack, layer, *, depth, batch, mem_len):
    d = mem2.shape[1]
    hd = d // X_HEADS
    chained = k_stack is not None
    lead = [k_stack, v_stack] if chained else []
    kern = functools.partial(_mem_kv_kernel, chained=chained)
    stack_shape = jax.ShapeDtypeStruct((depth, batch, mem_len, X_HEADS, hd), F32)
    spec5 = pl.BlockSpec((None, None, mem_len, X_HEADS, hd), lambda b: (layer, b, 0, 0, 0))
    return pl.pallas_call(
        kern,
        grid=(batch,),
        in_specs=[pl.BlockSpec(memory_space=pl.ANY)] * len(lead) + [
            pl.BlockSpec((mem_len, d), lambda b: (b, 0)),
            pl.BlockSpec((None, d, d), lambda b: (layer, 0, 0)),
            pl.BlockSpec((None, d, d), lambda b: (layer, 0, 0)),
        ],
        out_specs=[
            pl.BlockSpec((mem_len, d), lambda b: (b, 0)),
            pl.BlockSpec((mem_len, d), lambda b: (b, 0)),
            spec5, spec5,
        ],
        out_shape=[
            jax.ShapeDtypeStruct((batch * mem_len, d), BF16),
            jax.ShapeDtypeStruct((batch * mem_len, d), BF16),
            stack_shape, stack_shape,
        ],
        input_output_aliases={0: 2, 1: 3} if chained else {},
        compiler_params=_params(("arbitrary",)),
        name="mem_kv",
    )(*lead, mem2, w_k, w_v)


def _xattn_prompt_kernel(x_ref, g_ref, wq_ref, k_ref, v_ref, wo_ref, o_ref, att_scr):
    hd = x_ref.shape[1] // X_HEADS
    x = x_ref[...]
    xb = _rms(x, g_ref[...]).astype(BF16)
    heads = [slice(h * hd, (h + 1) * hd) for h in range(X_HEADS)]
    q = [jnp.dot(xb, wq_ref[:, hs], preferred_element_type=F32).astype(BF16) for hs in heads]
    s = [_dot_nt(q[h], k_ref[:, hs]) * (hd ** -0.5) for h, hs in enumerate(heads)]
    p = [_softmax_rows(sh).astype(BF16) for sh in s]
    for h, hs in enumerate(heads):
        att_scr[:, hs] = _dot(p[h], v_ref[:, hs]).astype(BF16)
    o_ref[...] = x + jnp.dot(att_scr[...], wo_ref[...], preferred_element_type=F32)


def _xattn_prompt(x, g, w_q, mk, mv, w_o, layer, *, batch, seq, mem_len):
    n_all, d = x.shape
    tq = _pick_tile(seq, ROW_TILE)
    nq = seq // tq
    return pl.pallas_call(
        _xattn_prompt_kernel,
        grid=(batch, nq),
        in_specs=[
            pl.BlockSpec((tq, d), lambda b, t: (b * nq + t, 0)),
            pl.BlockSpec((None, 1, d), lambda b, t: (layer, 0, 0)),
            pl.BlockSpec((None, d, d), lambda b, t: (layer, 0, 0)),
            pl.BlockSpec((mem_len, d), lambda b, t: (b, 0)),
            pl.BlockSpec((mem_len, d), lambda b, t: (b, 0)),
            pl.BlockSpec((None, d, d), lambda b, t: (layer, 0, 0)),
        ],
        out_specs=pl.BlockSpec((tq, d), lambda b, t: (b * nq + t, 0)),
        out_shape=jax.ShapeDtypeStruct((n_all, d), F32),
        scratch_shapes=[pltpu.VMEM((tq, d), BF16)],
        compiler_params=_params(("arbitrary", "arbitrary")),
        name="xattn_prompt",
    )(x, g, w_q, mk, mv, w_o)


def _attn_sample_kernel(q_ref, k_ref, v_ref, o_ref, *, seq, spb):
    mem_len, nh, hd = k_ref.shape[1:]
    lseq = seq.bit_length() - 1
    rows_h = _iota2((nh * seq, mem_len * nh), 0) >> lseq
    cols_h = _iota2((nh * seq, mem_len * nh), 1) & (nh - 1)
    own_head = rows_h == cols_h
    seqs = range(spb)
    q_all = []
    for s in seqs:
        q = q_ref[s * seq:(s + 1) * seq, :]
        q_all.append(jnp.concatenate([q[:, h * hd:(h + 1) * hd] for h in range(nh)], axis=0))
    sc = [jnp.where(own_head, _dot_nt(q_all[s], k_ref[s].reshape(mem_len * nh, hd)) * (hd ** -0.5), -jnp.inf)
          for s in seqs]
    p = [_softmax_rows(x).astype(BF16) for x in sc]
    for s in seqs:
        o_all = _dot(p[s], v_ref[s].reshape(mem_len * nh, hd))
        for h in range(nh):
            o_ref[s * seq:(s + 1) * seq, h * hd:(h + 1) * hd] = o_all[h * seq:(h + 1) * seq].astype(o_ref.dtype)


def _attn_sample(q, cache_k, cache_v, layer, *, n_seq, seq):
    d = q.shape[1]
    _, _, mem_len, nh, hd = cache_k.shape
    assert nh & (nh - 1) == 0
    spb = 8
    assert n_seq % spb == 0
    rows = spb * seq
    kern = functools.partial(_attn_sample_kernel, seq=seq, spb=spb)
    return pl.pallas_call(
        kern,
        grid=(n_seq // spb,),
        in_specs=[
            pl.BlockSpec((rows, d), lambda i: (i, 0)),
            pl.BlockSpec((None, spb, mem_len, nh, hd), lambda i: (layer, i, 0, 0, 0)),
            pl.BlockSpec((None, spb, mem_len, nh, hd), lambda i: (layer, i, 0, 0, 0)),
        ],
        out_specs=pl.BlockSpec((rows, d), lambda i: (i, 0)),
        out_shape=jax.ShapeDtypeStruct(q.shape, BF16),
        compiler_params=_params(("arbitrary",)),
        name="attn_sample",
    )(q, cache_k, cache_v)


def _gate_lane_vector(v, lane0):
    depth, n = v.shape
    out = jnp.zeros((depth, 1, LANES), F32)
    return out.at[:, 0, lane0:lane0 + n].set(v.astype(F32))


def kernel(x_prompt, x_sample, mem_prompt, state_gdn, state_qkv_conv, state_short_conv, cache_mem_k, cache_mem_v, g_ffn1, w_ffn1_gu, w_ffn1_down, g_mix, w_in, conv_qkv_w, a_log, dt_bias, g_gdn_out, sconv_w, w_out, g_xattn, w_xq, w_xk, w_xv, w_xo, g_ffn2, w_ffn2_gu, w_ffn2_down, g_final):
    batch, seq, d = x_prompt.shape
    dec_batch, dec_seq, _ = x_sample.shape
    depth = w_in.shape[0]
    mem_len = mem_prompt.shape[1]
    scw = sconv_w.shape[2]
    n_p = batch * seq
    n_s = dec_batch * dec_seq
    off_beta = QKV_WIDTH + GDN_V
    off_sc = off_beta + 2 * GDN_HEADS

    w_in_p = jnp.concatenate(
        [w_in[:, :, :off_beta], w_in[:, :, off_sc:], w_in[:, :, off_beta:off_sc],
         jnp.zeros((depth, d, LANES - 2 * GDN_HEADS), w_in.dtype)], axis=2).astype(BF16)
    wgu1, wd1 = w_ffn1_gu.astype(BF16), w_ffn1_down.astype(BF16)
    wgu2, wd2 = w_ffn2_gu.astype(BF16), w_ffn2_down.astype(BF16)
    w_out_b, w_xq_b, w_xo_b = w_out.astype(BF16), w_xq.astype(BF16), w_xo.astype(BF16)
    w_xk_b, w_xv_b = w_xk.astype(BF16), w_xv.astype(BF16)
    alog_v = _gate_lane_vector(a_log, GATE_A_LANE)
    dtb_v = _gate_lane_vector(dt_bias, GATE_A_LANE)
    g1, gm, gx, g2 = (t.reshape(depth, 1, d) for t in (g_ffn1, g_mix, g_xattn, g_ffn2))
    gn = g_gdn_out.reshape(depth, 1, GDN_DV)
    gf = g_final.reshape(1, d)
    mem2 = mem_prompt.reshape(batch * mem_len, d)
    qbuf_all = jnp.pad(state_qkv_conv, ((0, 0), (0, 0), (0, dec_seq - (GDN_CONV - 1)), (0, 0)))
    cbuf_all = jnp.pad(state_short_conv, ((0, 0), (0, 0), (0, dec_seq - (SC_CONV - 1)), (0, 0)))

    p_s, p_qb, p_sb, s_qb, s_sb = [], [], [], [], []
    st_stack = k_stack = v_stack = None
    x = None
    for l in range(depth):
        xs = [x_prompt.reshape(n_p, d), x_sample.reshape(n_s, d)] if l == 0 else [x]
        x = _ffn(xs, g1, wgu1, wd1, gf, l, final_norm=False)

        x1, st_p, qb_p, chb_p = _mixer_pipe(x, gm, w_in_p, conv_qkv_w, sconv_w, alog_v, dtb_v, gn,
                                              w_out_b, l, batch=batch, seq=seq)
        x1, st_stack, up_s, ch_s = _mixer_sample(
            x1, x, gm, w_in_p, state_gdn, st_stack, qbuf_all[l].reshape(n_s, QKV_WIDTH),
            cbuf_all[l].reshape(n_s, scw), conv_qkv_w, sconv_w, alog_v, dtb_v, gn, w_out_b, l,
            row0=n_p, n_seq=dec_batch, seq=dec_seq)

        mk, mv, k_stack, v_stack = _mem_kv(mem2, w_xk_b, w_xv_b, k_stack, v_stack, l,
                                           depth=depth, batch=batch, mem_len=mem_len)
        x2 = _xattn_prompt(x1, gx, w_xq_b, mk, mv, w_xo_b, l, batch=batch, seq=seq, mem_len=mem_len)
        q_s = _matmul(x1, w_xq_b, l, n_rows=n_s, x_row0=n_p, g=gx, out_dtype=BF16, name="xq_sample")
        att_s = _attn_sample(q_s, cache_mem_k, cache_mem_v, l, n_seq=dec_batch, seq=dec_seq)
        x2 = _matmul(att_s, w_xo_b, l, n_rows=n_s, res=x1, res_row0=n_p, into=x2, out_row0=n_p,
                     name="xo_sample")

        if l == depth - 1:
            y_prompt, y_sample = _ffn([x2], g2, wgu2, wd2, gf, l, final_norm=True, split_rows=n_p)
        else:
            x = _ffn([x2], g2, wgu2, wd2, gf, l, final_norm=False)

        p_s.append(st_p)
        p_qb.append(qb_p[:, SUBLANES - (GDN_CONV - 1):, :])
        p_sb.append(chb_p[:, SUBLANES - (SC_CONV - 1):, :])
        s_qb.append(up_s.reshape(dec_batch, dec_seq, QKV_WIDTH)[:, dec_seq - (GDN_CONV - 1):, :])
        s_sb.append(ch_s.reshape(dec_batch, dec_seq, scw)[:, dec_seq - (SC_CONV - 1):, :])

    return (y_prompt.reshape(batch, seq, d), y_sample.reshape(dec_batch, dec_seq, d),
            jnp.stack(p_s), jnp.stack(p_qb), jnp.stack(p_sb), k_stack, v_stack,
            st_stack, jnp.stack(s_qb), jnp.stack(s_sb))
```

```python
import functools

import jax
import jax.numpy as jnp
from jax import lax
from jax.experimental import pallas as pl
from jax.experimental.pallas import tpu as pltpu

F32 = jnp.float32
BF16 = jnp.bfloat16

GDN_HEADS = 4
GDN_DK = 128
GDN_DV = 128
GDN_QK = GDN_HEADS * GDN_DK
GDN_V = GDN_HEADS * GDN_DV
QKV_WIDTH = 2 * GDN_QK + GDN_V
GDN_CONV = 4
SC_CONV = 3
CHUNK = 64
MAT_TILE = 128
STAGE_ROWS = 512
X_HEADS = 4
RMS_EPS = 1e-6
LANES = 128
SUBLANES = 8
VMEM_LIMIT = 56 * 1024 * 1024
ROW_TILE = 512
FFN_ROW_TILE = 1024

COL_Z = QKV_WIDTH
COL_SCB = COL_Z + GDN_V
GATE_BETA_LANE = 0
GATE_A_LANE = GDN_HEADS


def _params(sem):
    return pltpu.CompilerParams(dimension_semantics=sem, vmem_limit_bytes=VMEM_LIMIT)


def _pick_tile(n, pref):
    t = min(pref, n)
    while n % t:
        t -= SUBLANES
    assert t > 0 and t % SUBLANES == 0
    return t


def _silu(x):
    return x * jax.nn.sigmoid(x)


def _rms(x, g):
    return x * lax.rsqrt(jnp.mean(x * x, axis=-1, keepdims=True) + RMS_EPS) * g


def _dot(a, b):
    return jnp.dot(a.astype(BF16), b.astype(BF16), preferred_element_type=F32)


def _dot_nt(a, b):
    return lax.dot_general(a.astype(BF16), b.astype(BF16), (((1,), (1,)), ((), ())),
                           preferred_element_type=F32)


def _iota2(shape, dim):
    return lax.broadcasted_iota(jnp.int32, shape, dim)


def _ffn_kernel(*refs, d_ff, col_chunk, final_norm, n_in, n_out, split_tile):
    x_refs = refs[:n_in]
    g_ref, wgu_ref, wd_ref, gf_ref = refs[n_in:n_in + 4]
    o_refs = refs[n_in + 4:n_in + 4 + n_out]
    h_ref = refs[n_in + 4 + n_out]
    i = pl.program_id(0)
    if n_in == 2:
        x = jnp.where(i < split_tile, x_refs[0][...], x_refs[1][...])
    else:
        x = x_refs[0][...]
    xb = _rms(x, g_ref[...]).astype(BF16)
    for c in range(d_ff // col_chunk):
        lo = c * col_chunk
        gate = jnp.dot(xb, wgu_ref[:, lo:lo + col_chunk], preferred_element_type=F32)
        up = jnp.dot(xb, wgu_ref[:, d_ff + lo:d_ff + lo + col_chunk], preferred_element_type=F32)
        h_ref[:, lo:lo + col_chunk] = (_silu(gate) * up).astype(BF16)
    y = x + 0.5 * jnp.dot(h_ref[...], wd_ref[...], preferred_element_type=F32)
    if final_norm:
        y = _rms(y, gf_ref[...])
    if n_out == 2:
        @pl.when(i < split_tile)
        def _():
            o_refs[0][...] = y

        @pl.when(i >= split_tile)
        def _():
            o_refs[1][...] = y
    else:
        o_refs[0][...] = y


def _ffn(xs, g, wgu, wd, g_final, layer, *, final_norm, split_rows=None):
    d = xs[0].shape[1]
    n = sum(a.shape[0] for a in xs)
    d_ff = wd.shape[1]
    tm = FFN_ROW_TILE
    col_chunk = 256
    assert d_ff % col_chunk == 0 and all(a.shape[0] % tm == 0 for a in xs)
    if len(xs) == 2:
        split_tile = xs[0].shape[0] // tm
        x_specs = [pl.BlockSpec((tm, d), lambda i: (jnp.minimum(i, split_tile - 1), 0)),
                   pl.BlockSpec((tm, d), lambda i: (jnp.maximum(i - split_tile, 0), 0))]
    else:
        split_tile = None
        x_specs = [pl.BlockSpec((tm, d), lambda i: (i, 0))]
    if split_rows is not None:
        assert split_rows % tm == 0
        split_tile = split_rows // tm
        out_specs = [pl.BlockSpec((tm, d), lambda i: (jnp.minimum(i, split_tile - 1), 0)),
                     pl.BlockSpec((tm, d), lambda i: (jnp.maximum(i - split_tile, 0), 0))]
        out_shape = [jax.ShapeDtypeStruct((split_rows, d), F32),
                     jax.ShapeDtypeStruct((n - split_rows, d), F32)]
    else:
        out_specs = [pl.BlockSpec((tm, d), lambda i: (i, 0))]
        out_shape = [jax.ShapeDtypeStruct((n, d), F32)]
    kern = functools.partial(_ffn_kernel, d_ff=d_ff, col_chunk=col_chunk, final_norm=final_norm,
                             n_in=len(xs), n_out=len(out_specs), split_tile=split_tile)
    out = pl.pallas_call(
        kern,
        grid=(n // tm,),
        in_specs=x_specs + [
            pl.BlockSpec((None, 1, d), lambda i: (layer, 0, 0)),
            pl.BlockSpec((None, d, 2 * d_ff), lambda i: (layer, 0, 0), pipeline_mode=pl.Buffered(1)),
            pl.BlockSpec((None, d_ff, d), lambda i: (layer, 0, 0), pipeline_mode=pl.Buffered(1)),
            pl.BlockSpec((1, d), lambda i: (0, 0)),
        ],
        out_specs=out_specs,
        out_shape=out_shape,
        scratch_shapes=[pltpu.VMEM((tm, d_ff), BF16)],
        compiler_params=_params(("arbitrary",)),
        name="ffn",
    )(*xs, g, wgu, wd, g_final)
    return out if split_rows is not None else out[0]


_DONE = object()


def _co_schedule(main, filler):
    for n in main:
        for _ in range(n or 0):
            next(filler, _DONE)
    for _ in filler:
        pass


def _run(stream):
    while True:
        try:
            next(stream)
        except StopIteration as stop:
            return stop.value


def _unit_lower_inverse_steps(a_list, merge_levels, fill=None):
    n = a_list[0].shape[0]
    r = _iota2((n, n), 0)
    c = _iota2((n, n), 1)
    eye = (r == c).astype(F32)
    blk8 = (r >> 3) == (c >> 3)
    d = [jnp.where(blk8, a, 0.0) for a in a_list]
    d2 = [_dot(x, x) for x in d]
    yield fill
    p = [_dot(eye - x, eye + y) for x, y in zip(d, d2)]
    d4 = [_dot(y, y) for y in d2]
    yield fill
    t = [_dot(x, eye + y) for x, y in zip(p, d4)]
    yield fill
    for lvl in merge_levels:
        mask = ((r >> (lvl + 1)) == (c >> (lvl + 1))) & ((r >> lvl) != (c >> lvl))
        lt = [_dot(jnp.where(mask, a, 0.0), x) for a, x in zip(a_list, t)]
        yield fill
        t = [x - _dot(x, y) for x, y in zip(t, lt)]
        yield fill
    return t


def _unit_lower_inverse(a_list, merge_levels):
    return _run(_unit_lower_inverse_steps(a_list, merge_levels))


def _l2n(t):
    return t * lax.rsqrt(jnp.sum(t * t, axis=-1, keepdims=True) + 1e-6)


def _segment_cumsum(g, seg):
    rin = _iota2(g.shape, 0) & (seg - 1)
    sh = 1
    while sh < seg:
        g = g + jnp.where(rin >= sh, pltpu.roll(g, sh, axis=0), 0.0)
        sh *= 2
    return g


def _segment_last(gc, seg):
    n = gc.shape[0]
    rin = _iota2(gc.shape, 0) & (seg - 1)
    x = jnp.where(rin == seg - 1, gc, 0.0)
    sh = 1
    while sh < seg:
        x = x + jnp.where(rin + sh <= seg - 1, pltpu.roll(x, n - sh, axis=0), 0.0)
        sh *= 2
    return x


def _gates(gate_blk, alog, dtb, seg):
    beta = jax.nn.sigmoid(gate_blk)
    xg = gate_blk + dtb
    softplus = jnp.maximum(xg, 0.0) + jnp.log1p(jnp.exp(-jnp.abs(xg)))
    g = -jnp.exp(alog) * softplus
    gc = _segment_cumsum(g, seg)
    return beta, gc


def _heads_prep(qkv, gate_blk, alog, dtb, seg):
    n = qkv.shape[0]
    mt = min(n, MAT_TILE)
    assert n % mt == 0 and mt % seg == 0
    tiles = [slice(j * mt, (j + 1) * mt) for j in range(n // mt)]
    lg = seg.bit_length() - 1
    beta, gc = _gates(gate_blk, alog, dtb, seg)
    egc = jnp.exp(gc)
    glast = _segment_last(gc, seg)
    ekt = jnp.exp(glast - gc)
    eglast = jnp.exp(glast)
    gc_t = gc.T
    r = _iota2((mt, mt), 0)
    c = _iota2((mt, mt), 1)
    same = (r >> lg) == (c >> lg)
    tri = (r >= c) & same
    strict = (r > c) & same
    heads = range(GDN_HEADS)
    q = [_l2n(qkv[:, h * GDN_DK:(h + 1) * GDN_DK]) * (GDN_DK ** -0.5) for h in heads]
    k = [_l2n(qkv[:, GDN_QK + h * GDN_DK:GDN_QK + (h + 1) * GDN_DK]) for h in heads]
    v = [qkv[:, 2 * GDN_QK + h * GDN_DV:2 * GDN_QK + (h + 1) * GDN_DV] for h in heads]
    bh = [beta[:, GATE_BETA_LANE + h:GATE_BETA_LANE + h + 1] for h in heads]
    eg = [egc[:, GATE_A_LANE + h:GATE_A_LANE + h + 1] for h in heads]
    kb = [k[h] * bh[h] for h in heads]
    rhs = [jnp.concatenate([v[h] * bh[h], kb[h] * eg[h]], axis=1) for h in heads]
    dec = [[None] * len(tiles) for _ in heads]
    for h in heads:
        al = GATE_A_LANE + h
        for j, rt in enumerate(tiles):
            diff = gc[rt, al:al + 1] - gc_t[al:al + 1, rt]
            dec[h][j] = jnp.where(tri, jnp.exp(jnp.where(tri, diff, 0.0)), 0.0)
    systems = [(h, j) for h in heads for j in range(len(tiles))]
    a = [jnp.where(strict, _dot_nt(kb[h][tiles[j]], k[h][tiles[j]]) * dec[h][j], 0.0)
         for h, j in systems]
    tinv = _unit_lower_inverse(a, merge_levels=tuple(range(3, lg)))
    uw = [_dot(t, rhs[h][tiles[j]]) for t, (h, j) in zip(tinv, systems)]
    uw = [jnp.concatenate(uw[h * len(tiles):(h + 1) * len(tiles)], axis=0) for h in heads]
    u = [x[:, :GDN_DV] for x in uw]
    w = [x[:, GDN_DV:] for x in uw]
    qk = [[_dot_nt(q[h][rt], k[h][rt]) * dec[h][j] for j, rt in enumerate(tiles)] for h in heads]
    qd = [q[h] * eg[h] for h in heads]
    kt = [k[h] * ekt[:, GATE_A_LANE + h:GATE_A_LANE + h + 1] for h in heads]
    return u, w, qk, qd, kt, eglast, tiles


def _gated_out_norm(o, z, gn):
    return _rms(o, gn) * _silu(z)


def _in_proj_cols(x, g_ref, win_ref):
    xb = _rms(x, g_ref[...]).astype(BF16)

    def cols(lo, width):
        return jnp.dot(xb, win_ref[:, lo:lo + width], preferred_element_type=F32)

    return cols


def _in_proj(x, g_ref, win_ref, scw):
    cols = _in_proj_cols(x, g_ref, win_ref)
    return (cols(0, QKV_WIDTH), cols(COL_Z, GDN_V), cols(COL_SCB, scw), cols(COL_SCB + scw, scw),
            cols(COL_SCB + 2 * scw, scw), cols(COL_SCB + 3 * scw, LANES))


OPS_Q, OPS_K, OPS_KB, OPS_QD, OPS_RHS = 0, GDN_QK, 2 * GDN_QK, 3 * GDN_QK, 4 * GDN_QK
OPS_WIDTH = 4 * GDN_QK + GDN_HEADS * 2 * GDN_DV
FILL_PER_CHUNK = 1
PROJ_SLAB = 256


def _mixer_pipe_kernel(xa_ref, xb_ref, g_ref, win_ref, cw_ref, scw_ref, alog_ref, dtb_ref, gn_ref, wout_ref,
                       o_ref, st_ref, qb_ref, chb_ref,
                       s_scr, extq, extc, mix_scr, ops_new, ops_old, kt_new, kt_old, xn_new, xn_old,
                       gate_new, gate_old, gct_new, gct_old, *, lt, nt, n_tiles):
    i = pl.program_id(0)
    ta = jnp.minimum(i, n_tiles - 1) % nt
    tb = jnp.maximum(i - 1, 0) % nt
    heads = range(GDN_HEADS)
    scw = extc.shape[1]
    mt = min(lt, MAT_TILE)
    tiles = [slice(j * mt, (j + 1) * mt) for j in range(lt // mt)]
    lg = CHUNK.bit_length() - 1

    @pl.when(i == 0)
    def _():
        for ref in (ops_old, kt_old, xn_old, gate_old, gct_old):
            ref[...] = jnp.zeros_like(ref)

    @pl.when(ta == 0)
    def _():
        extq[0:SUBLANES, :] = jnp.zeros((SUBLANES, QKV_WIDTH), F32)

    @pl.when(tb == 0)
    def _():
        s_scr[...] = jnp.zeros_like(s_scr)
        extc[0:SUBLANES, :] = jnp.zeros((SUBLANES, scw), F32)

    late = {}
    pending = [("b", COL_SCB), ("c", COL_SCB + scw), ("h", COL_SCB + 2 * scw)]
    ch_tail = []

    def cols_b(lo, width):
        return jnp.dot(xn_old[...], win_ref[:, lo:lo + width], preferred_element_type=F32)

    def stage_a():
        xn = _rms(xa_ref[...], g_ref[...]).astype(BF16)
        xn_new[...] = xn
        gate_blk = jnp.dot(xn, win_ref[:, COL_SCB + 3 * scw:COL_SCB + 3 * scw + LANES],
                           preferred_element_type=F32)
        beta, gc = _gates(gate_blk, alog_ref[...], dtb_ref[...], CHUNK)
        egc = jnp.exp(gc)
        glast = _segment_last(gc, CHUNK)
        ekt = jnp.exp(glast - gc)
        gate_new[:, 0:LANES] = gc
        gate_new[:, LANES:2 * LANES] = jnp.exp(glast)
        gct_new[...] = gc.T
        yield
        act = []
        for slab in range(QKV_WIDTH // PROJ_SLAB):
            sl = slice(slab * PROJ_SLAB, (slab + 1) * PROJ_SLAB)
            extq[SUBLANES:SUBLANES + lt, sl] = jnp.dot(xn, win_ref[:, sl], preferred_element_type=F32)
            for gcol in range(slab * PROJ_SLAB // LANES, (slab + 1) * PROJ_SLAB // LANES):
                cs = slice(gcol * LANES, (gcol + 1) * LANES)
                conv = cw_ref[GDN_CONV - 1:GDN_CONV, cs] * extq[SUBLANES:SUBLANES + lt, cs]
                for s in range(1, GDN_CONV):
                    conv = conv + cw_ref[GDN_CONV - 1 - s:GDN_CONV - s, cs] * extq[SUBLANES - s:SUBLANES - s + lt, cs]
                act.append(_silu(conv))
            yield
        for h in heads:
            hs = slice(h * GDN_DK, (h + 1) * GDN_DK)
            bh = beta[:, GATE_BETA_LANE + h:GATE_BETA_LANE + h + 1]
            eg = egc[:, GATE_A_LANE + h:GATE_A_LANE + h + 1]
            q = _l2n(act[h]) * (GDN_DK ** -0.5)
            k = _l2n(act[GDN_HEADS + h])
            kb = k * bh
            ops_new[:, OPS_Q + h * GDN_DK:OPS_Q + (h + 1) * GDN_DK] = q.astype(BF16)
            ops_new[:, OPS_K + h * GDN_DK:OPS_K + (h + 1) * GDN_DK] = k.astype(BF16)
            ops_new[:, OPS_KB + h * GDN_DK:OPS_KB + (h + 1) * GDN_DK] = kb.astype(BF16)
            ops_new[:, OPS_QD + h * GDN_DK:OPS_QD + (h + 1) * GDN_DK] = (q * eg).astype(BF16)
            lo = OPS_RHS + h * 2 * GDN_DV
            ops_new[:, lo:lo + GDN_DV] = (act[2 * GDN_HEADS + h] * bh).astype(BF16)
            ops_new[:, lo + GDN_DV:lo + 2 * GDN_DV] = (kb * eg).astype(BF16)
            kt_new[:, hs] = k * ekt[:, GATE_A_LANE + h:GATE_A_LANE + h + 1]
            yield

    def stage_b():
        yield 1
        r = _iota2((mt, mt), 0)
        c = _iota2((mt, mt), 1)
        same = (r >> lg) == (c >> lg)
        tri = (r >= c) & same
        strict = (r > c) & same
        systems = [(h, j) for h in heads for j in range(len(tiles))]

        def op(lo, h, rows, width=GDN_DK):
            return ops_old[rows, lo + h * width:lo + (h + 1) * width]

        dec = []
        for h, j in systems:
            al = GATE_A_LANE + h
            diff = gate_old[tiles[j], al:al + 1] - gct_old[al:al + 1, tiles[j]]
            dec.append(jnp.where(tri, jnp.exp(jnp.where(tri, diff, 0.0)), 0.0))
        yield
        a = [jnp.where(strict, _dot_nt(op(OPS_KB, h, tiles[j]), op(OPS_K, h, tiles[j])) * dk, 0.0)
             for (h, j), dk in zip(systems, dec)]
        yield
        tinv = yield from _unit_lower_inverse_steps(a, merge_levels=tuple(range(3, lg)), fill=1)
        uw = [_dot(tk, op(OPS_RHS, h, tiles[j], 2 * GDN_DV)) for (h, j), tk in zip(systems, tinv)]
        uw = [jnp.concatenate(uw[h * len(tiles):(h + 1) * len(tiles)], axis=0) for h in heads]
        yield 1

        def short_conv_group(gcol):
            cs = slice(gcol * LANES, (gcol + 1) * LANES)
            ch = late["c"][:, cs] * late["h"][:, cs]
            extc[SUBLANES:SUBLANES + lt, cs] = ch
            y = scw_ref[SC_CONV - 1:SC_CONV, cs] * ch
            for s in range(1, SC_CONV):
                y = y + scw_ref[SC_CONV - 1 - s:SC_CONV - s, cs] * extc[SUBLANES - s:SUBLANES - s + lt, cs]
            mix_scr[:, GDN_V + gcol * LANES:GDN_V + (gcol + 1) * LANES] = (late["b"][:, cs] * y).astype(BF16)
            ch_tail.append(ch[lt - SUBLANES:lt, :])

        sc_groups = list(range(scw // LANES))
        n_chunks = lt // CHUNK
        qk = []
        qk_per_chunk = -(-len(systems) // n_chunks)
        s_cur = [s_scr[h] for h in heads]
        v_parts = [[] for _ in heads]
        o_parts = [[] for _ in heads]
        for cc in range(n_chunks):
            row0 = cc * CHUNK
            rs = slice(row0, row0 + CHUNK)
            ws = [_dot(jnp.concatenate([uw[h][rs, GDN_DV:], op(OPS_QD, h, rs)], axis=0), s_cur[h])
                  for h in heads]
            vn = [uw[h][rs, :GDN_DV] - ws[h][:CHUNK] for h in heads]
            s_cur = [s_cur[h] * gate_old[row0:row0 + 1, LANES + GATE_A_LANE + h:LANES + GATE_A_LANE + h + 1]
                     + _dot(kt_old[rs, h * GDN_DK:(h + 1) * GDN_DK].T, vn[h]) for h in heads]
            for h in heads:
                v_parts[h].append(vn[h])
                o_parts[h].append(ws[h][CHUNK:])
            if "z" not in late:
                late["z"] = cols_b(COL_Z, GDN_V)
            elif pending:
                name, lo = pending.pop(0)
                late[name] = cols_b(lo, scw)
            elif sc_groups:
                short_conv_group(sc_groups.pop(0))
            for (h, j), dk in list(zip(systems, dec))[len(qk):len(qk) + qk_per_chunk]:
                qk.append(_dot_nt(op(OPS_Q, h, tiles[j]), op(OPS_K, h, tiles[j])) * dk)
            yield FILL_PER_CHUNK
        while pending:
            name, lo = pending.pop(0)
            late[name] = cols_b(lo, scw)
        while sc_groups:
            short_conv_group(sc_groups.pop(0))
        for h in heads:
            s_scr[h] = s_cur[h]
            v_new = jnp.concatenate(v_parts[h], axis=0)
            o = jnp.concatenate(o_parts[h], axis=0) + jnp.concatenate(
                [_dot(qk[h * len(tiles) + j], v_new[rt]) for j, rt in enumerate(tiles)], axis=0)
            hs = slice(h * GDN_DV, (h + 1) * GDN_DV)
            mix_scr[:, hs] = _gated_out_norm(o, late["z"][:, hs], gn_ref[...]).astype(BF16)
            yield

    _co_schedule(stage_b(), stage_a())

    o_ref[...] = xb_ref[...] + jnp.dot(mix_scr[...], wout_ref[...], preferred_element_type=F32)

    u_tail = extq[lt:lt + SUBLANES, :]
    c_tail = jnp.concatenate(ch_tail, axis=1)
    extq[0:SUBLANES, :] = u_tail
    extc[0:SUBLANES, :] = c_tail
    for new, old in ((ops_new, ops_old), (kt_new, kt_old), (xn_new, xn_old), (gate_new, gate_old),
                     (gct_new, gct_old)):
        old[...] = new[...]

    @pl.when(ta == nt - 1)
    def _():
        qb_ref[...] = u_tail

    @pl.when(tb == nt - 1)
    def _():
        st_ref[...] = s_scr[...]
        chb_ref[...] = c_tail


def _mixer_pipe(x, g, w_in, conv_w, sconv_w, alog_v, dtb_v, g_norm, w_out, layer, *, batch, seq):
    n_all, d = x.shape
    lt = _pick_tile(seq, STAGE_ROWS)
    assert lt % CHUNK == 0
    nt = seq // lt
    n_tiles = batch * nt
    cols = w_in.shape[2]
    scw = sconv_w.shape[2]
    kern = functools.partial(_mixer_pipe_kernel, lt=lt, nt=nt, n_tiles=n_tiles)

    def tile_a(i):
        return jnp.minimum(i, n_tiles - 1)

    def tile_b(i):
        return jnp.maximum(i - 1, 0)

    return pl.pallas_call(
        kern,
        grid=(n_tiles + 1,),
        in_specs=[
            pl.BlockSpec((lt, d), lambda i: (tile_a(i), 0)),
            pl.BlockSpec((lt, d), lambda i: (tile_b(i), 0)),
            pl.BlockSpec((None, 1, d), lambda i: (layer, 0, 0)),
            pl.BlockSpec((None, d, cols), lambda i: (layer, 0, 0)),
            pl.BlockSpec((None, GDN_CONV, QKV_WIDTH), lambda i: (layer, 0, 0)),
            pl.BlockSpec((None, SC_CONV, scw), lambda i: (layer, 0, 0)),
            pl.BlockSpec((None, 1, LANES), lambda i: (layer, 0, 0)),
            pl.BlockSpec((None, 1, LANES), lambda i: (layer, 0, 0)),
            pl.BlockSpec((None, 1, GDN_DV), lambda i: (layer, 0, 0)),
            pl.BlockSpec((None, GDN_V + scw, d), lambda i: (layer, 0, 0)),
        ],
        out_specs=[
            pl.BlockSpec((lt, d), lambda i: (tile_b(i), 0)),
            pl.BlockSpec((None, GDN_HEADS, GDN_DK, GDN_DV), lambda i: (tile_b(i) // nt, 0, 0, 0)),
            pl.BlockSpec((None, SUBLANES, QKV_WIDTH), lambda i: (tile_a(i) // nt, 0, 0)),
            pl.BlockSpec((None, SUBLANES, scw), lambda i: (tile_b(i) // nt, 0, 0)),
        ],
        out_shape=[
            jax.ShapeDtypeStruct((n_all, d), F32),
            jax.ShapeDtypeStruct((batch, GDN_HEADS, GDN_DK, GDN_DV), F32),
            jax.ShapeDtypeStruct((batch, SUBLANES, QKV_WIDTH), F32),
            jax.ShapeDtypeStruct((batch, SUBLANES, scw), F32),
        ],
        scratch_shapes=[
            pltpu.VMEM((GDN_HEADS, GDN_DK, GDN_DV), F32),
            pltpu.VMEM((lt + SUBLANES, QKV_WIDTH), F32),
            pltpu.VMEM((lt + SUBLANES, scw), F32),
            pltpu.VMEM((lt, GDN_V + scw), BF16),
            pltpu.VMEM((lt, OPS_WIDTH), BF16), pltpu.VMEM((lt, OPS_WIDTH), BF16),
            pltpu.VMEM((lt, GDN_QK), F32), pltpu.VMEM((lt, GDN_QK), F32),
            pltpu.VMEM((lt, d), BF16), pltpu.VMEM((lt, d), BF16),
            pltpu.VMEM((lt, 2 * LANES), F32), pltpu.VMEM((lt, 2 * LANES), F32),
            pltpu.VMEM((LANES, lt), F32), pltpu.VMEM((LANES, lt), F32),
        ],
        compiler_params=_params(("arbitrary",)),
        name="mixer_prompt",
    )(x, x, g, w_in, conv_w, sconv_w, alog_v, dtb_v, g_norm, w_out)


def _shift_in_segments(x, buf, s, buf_rows, seg):
    n = x.shape[0]
    rin = _iota2(x.shape, 0) & (seg - 1)
    from_x = pltpu.roll(x, s, axis=0)
    back = (n + s - buf_rows) % n
    from_buf = pltpu.roll(buf, back, axis=0) if back else buf
    return jnp.where(rin >= s, from_x, from_buf)


def _mixer_sample_kernel(*refs, seq, chained):
    it = iter(refs)
    next(it)
    if chained:
        next(it)
    (x_ref, g_ref, win_ref, st_in_ref, qbuf_ref, cbuf_ref, cw_ref, scw_ref, alog_ref, dtb_ref,
     gn_ref, wout_ref, o_ref, st_ref, up_ref, ch_ref, mix_scr) = it
    n = x_ref.shape[0]
    nseq = n // seq
    lseq = seq.bit_length() - 1
    heads = range(GDN_HEADS)
    scw = cbuf_ref.shape[1]

    x = x_ref[...]
    u_pre, z, b_gate, c_gate, h_gate, gate_blk = _in_proj(x, g_ref, win_ref, scw)
    qbuf = qbuf_ref[...]
    conv = cw_ref[GDN_CONV - 1:GDN_CONV, :] * u_pre
    for s in range(1, GDN_CONV):
        conv = conv + cw_ref[GDN_CONV - 1 - s:GDN_CONV - s, :] * _shift_in_segments(
            u_pre, qbuf, s, GDN_CONV - 1, seq)
    qkv = _silu(conv)

    u, w, qk, qd, kt, eglast, tiles = _heads_prep(qkv, gate_blk, alog_ref[...], dtb_ref[...], seq)

    rseq = _iota2((n, GDN_DV), 0) >> lseq
    kt_t = [kt[h].T for h in heads]
    v_parts = [[] for _ in heads]
    o_parts = [[] for _ in heads]
    for s in range(nseq):
        rs = slice(s * seq, (s + 1) * seq)
        ws = [_dot(jnp.concatenate([w[h][rs], qd[h][rs]], axis=0), st_in_ref[s, h]) for h in heads]
        for h in heads:
            v_parts[h].append(u[h][rs] - ws[h][:seq])
            o_parts[h].append(ws[h][seq:])
    v_new = [jnp.concatenate(v_parts[h], axis=0) for h in heads]
    for h in heads:
        hs = slice(h * GDN_DV, (h + 1) * GDN_DV)
        o = jnp.concatenate(o_parts[h], axis=0) + jnp.concatenate(
            [_dot(qk[h][j], v_new[h][rt]) for j, rt in enumerate(tiles)], axis=0)
        mix_scr[:, hs] = _gated_out_norm(o, z[:, hs], gn_ref[...]).astype(BF16)
    for s in range(nseq):
        for h in heads:
            v_s = jnp.where(rseq == s, v_new[h], 0.0)
            st_ref[s, h] = (st_in_ref[s, h]
                            * eglast[s * seq:s * seq + 1, GATE_A_LANE + h:GATE_A_LANE + h + 1]
                            + _dot(kt_t[h], v_s))

    ch = c_gate * h_gate
    cbuf = cbuf_ref[...]
    y = scw_ref[SC_CONV - 1:SC_CONV, :] * ch
    for s in range(1, SC_CONV):
        y = y + scw_ref[SC_CONV - 1 - s:SC_CONV - s, :] * _shift_in_segments(ch, cbuf, s, SC_CONV - 1, seq)
    mix_scr[:, GDN_V:GDN_V + scw] = (b_gate * y).astype(BF16)
    o_ref[...] = x + jnp.dot(mix_scr[...], wout_ref[...], preferred_element_type=F32)
    up_ref[...] = u_pre
    ch_ref[...] = ch


def _mixer_sample(x_new, x, g, w_in, state, st_stack, qbuf, cbuf, conv_w, sconv_w, alog_v, dtb_v, g_norm,
                  w_out, layer, *, row0, n_seq, seq):
    n_all, d = x.shape
    depth = state.shape[0]
    assert seq == SUBLANES
    rows = _pick_tile(n_seq * seq, 128)
    spb = rows // seq
    assert n_seq % spb == 0 and row0 % rows == 0
    blk0 = row0 // rows
    cols = w_in.shape[2]
    scw = sconv_w.shape[2]
    chained = st_stack is not None
    kern = functools.partial(_mixer_sample_kernel, seq=seq, chained=chained)
    lead = [x_new] + ([st_stack] if chained else [])
    aliases = {0: 0, 1: 1} if chained else {0: 0}
    return pl.pallas_call(
        kern,
        grid=(n_seq // spb,),
        in_specs=[pl.BlockSpec(memory_space=pl.ANY)] * len(lead) + [
            pl.BlockSpec((rows, d), lambda i: (blk0 + i, 0)),
            pl.BlockSpec((None, 1, d), lambda i: (layer, 0, 0)),
            pl.BlockSpec((None, d, cols), lambda i: (layer, 0, 0)),
            pl.BlockSpec((None, spb, GDN_HEADS, GDN_DK, GDN_DV), lambda i: (layer, i, 0, 0, 0)),
            pl.BlockSpec((rows, QKV_WIDTH), lambda i: (i, 0)),
            pl.BlockSpec((rows, scw), lambda i: (i, 0)),
            pl.BlockSpec((None, GDN_CONV, QKV_WIDTH), lambda i: (layer, 0, 0)),
            pl.BlockSpec((None, SC_CONV, scw), lambda i: (layer, 0, 0)),
            pl.BlockSpec((None, 1, LANES), lambda i: (layer, 0, 0)),
            pl.BlockSpec((None, 1, LANES), lambda i: (layer, 0, 0)),
            pl.BlockSpec((None, 1, GDN_DV), lambda i: (layer, 0, 0)),
            pl.BlockSpec((None, GDN_V + scw, d), lambda i: (layer, 0, 0)),
        ],
        out_specs=[
            pl.BlockSpec((rows, d), lambda i: (blk0 + i, 0)),
            pl.BlockSpec((None, spb, GDN_HEADS, GDN_DK, GDN_DV), lambda i: (layer, i, 0, 0, 0)),
            pl.BlockSpec((rows, QKV_WIDTH), lambda i: (i, 0)),
            pl.BlockSpec((rows, scw), lambda i: (i, 0)),
        ],
        out_shape=[
            jax.ShapeDtypeStruct((n_all, d), F32),
            jax.ShapeDtypeStruct((depth, n_seq, GDN_HEADS, GDN_DK, GDN_DV), F32),
            jax.ShapeDtypeStruct((n_seq * seq, QKV_WIDTH), F32),
            jax.ShapeDtypeStruct((n_seq * seq, scw), F32),
        ],
        scratch_shapes=[pltpu.VMEM((rows, GDN_V + scw), BF16)],
        input_output_aliases=aliases,
        compiler_params=_params(("arbitrary",)),
        name="mixer_sample",
    )(*lead, x, g, w_in, state, qbuf, cbuf, conv_w, sconv_w, alog_v, dtb_v, g_norm, w_out)


def _softmax_rows(s):
    m = jnp.max(s, axis=-1, keepdims=True)
    p = jnp.exp(s - m)
    return p / jnp.sum(p, axis=-1, keepdims=True)


def _mem_kv_kernel(*refs, chained):
    it = iter(refs)
    if chained:
        next(it)
        next(it)
    mem_ref, wk_ref, wv_ref, kb_ref, vb_ref, k5_ref, v5_ref = it
    nh, hd = k5_ref.shape[1:]
    mb = mem_ref[...].astype(BF16)
    for w_ref, b_ref, o5_ref in ((wk_ref, kb_ref, k5_ref), (wv_ref, vb_ref, v5_ref)):
        acc = jnp.dot(mb, w_ref[...], preferred_element_type=F32)
        b_ref[...] = acc.astype(BF16)
        for h in range(nh):
            o5_ref[:, h, :] = acc[:, h * hd:(h + 1) * hd]


def _mem_kv(mem2, w_k, w_v, k_stack, v_stack, layer, *, depth, batch, mem_len):
    d = mem2.shape[1]
    hd = d // X_HEADS
    chained = k_stack is not None
    lead = [k_stack, v_stack] if chained else []
    kern = functools.partial(_mem_kv_kernel, chained=chained)
    stack_shape = jax.ShapeDtypeStruct((depth, batch, mem_len, X_HEADS, hd), F32)
    spec5 = pl.BlockSpec((None, None, mem_len, X_HEADS, hd), lambda b: (layer, b, 0, 0, 0))
    return pl.pallas_call(
        kern,
        grid=(batch,),
        in_specs=[pl.BlockSpec(memory_space=pl.ANY)] * len(lead) + [
            pl.BlockSpec((mem_len, d), lambda b: (b, 0)),
            pl.BlockSpec((None, d, d), lambda b: (layer, 0, 0)),
            pl.BlockSpec((None, d, d), lambda b: (layer, 0, 0)),
        ],
        out_specs=[
            pl.BlockSpec((mem_len, d), lambda b: (b, 0)),
            pl.BlockSpec((mem_len, d), lambda b: (b, 0)),
            spec5, spec5,
        ],
        out_shape=[
            jax.ShapeDtypeStruct((batch * mem_len, d), BF16),
            jax.ShapeDtypeStruct((batch * mem_len, d), BF16),
            stack_shape, stack_shape,
        ],
        input_output_aliases={0: 2, 1: 3} if chained else {},
        compiler_params=_params(("arbitrary",)),
        name="mem_kv",
    )(*lead, mem2, w_k, w_v)


def _xattn_prompt_kernel(x_ref, g_ref, wq_ref, k_ref, v_ref, wo_ref, o_ref, att_scr):
    hd = x_ref.shape[1] // X_HEADS
    x = x_ref[...]
    xb = _rms(x, g_ref[...]).astype(BF16)
    heads = [slice(h * hd, (h + 1) * hd) for h in range(X_HEADS)]
    q = [jnp.dot(xb, wq_ref[:, hs], preferred_element_type=F32).astype(BF16) for hs in heads]
    s = [_dot_nt(q[h], k_ref[:, hs]) * (hd ** -0.5) for h, hs in enumerate(heads)]
    p = [_softmax_rows(sh).astype(BF16) for sh in s]
    for h, hs in enumerate(heads):
        att_scr[:, hs] = _dot(p[h], v_ref[:, hs]).astype(BF16)
    o_ref[...] = x + jnp.dot(att_scr[...], wo_ref[...], preferred_element_type=F32)


def _xattn_prompt(x, g, w_q, mk, mv, w_o, layer, *, batch, seq, mem_len):
    n_all, d = x.shape
    tq = _pick_tile(seq, ROW_TILE)
    nq = seq // tq
    return pl.pallas_call(
        _xattn_prompt_kernel,
        grid=(batch, nq),
        in_specs=[
            pl.BlockSpec((tq, d), lambda b, t: (b * nq + t, 0)),
            pl.BlockSpec((None, 1, d), lambda b, t: (layer, 0, 0)),
            pl.BlockSpec((None, d, d), lambda b, t: (layer, 0, 0)),
            pl.BlockSpec((mem_len, d), lambda b, t: (b, 0)),
            pl.BlockSpec((mem_len, d), lambda b, t: (b, 0)),
            pl.BlockSpec((None, d, d), lambda b, t: (layer, 0, 0)),
        ],
        out_specs=pl.BlockSpec((tq, d), lambda b, t: (b * nq + t, 0)),
        out_shape=jax.ShapeDtypeStruct((n_all, d), F32),
        scratch_shapes=[pltpu.VMEM((tq, d), BF16)],
        compiler_params=_params(("arbitrary", "arbitrary")),
        name="xattn_prompt",
    )(x, g, w_q, mk, mv, w_o)


def _xattn_sample_kernel(x_new_ref, x_ref, g_ref, wq_ref, k_ref, v_ref, wo_ref, o_ref, att_scr, *, seq, spb):
    del x_new_ref
    mem_len, nh, hd = k_ref.shape[1:]
    lseq = seq.bit_length() - 1
    rows_h = _iota2((nh * seq, mem_len * nh), 0) >> lseq
    cols_h = _iota2((nh * seq, mem_len * nh), 1) & (nh - 1)
    own_head = rows_h == cols_h
    x = x_ref[...]
    q_rows = jnp.dot(_rms(x, g_ref[...]).astype(BF16), wq_ref[...], preferred_element_type=F32).astype(BF16)
    seqs = range(spb)
    q_all = []
    for s in seqs:
        q = q_rows[s * seq:(s + 1) * seq, :]
        q_all.append(jnp.concatenate([q[:, h * hd:(h + 1) * hd] for h in range(nh)], axis=0))
    sc = [jnp.where(own_head, _dot_nt(q_all[s], k_ref[s].reshape(mem_len * nh, hd)) * (hd ** -0.5), -jnp.inf)
          for s in seqs]
    p = [_softmax_rows(t).astype(BF16) for t in sc]
    for s in seqs:
        o_all = _dot(p[s], v_ref[s].reshape(mem_len * nh, hd))
        for h in range(nh):
            att_scr[s * seq:(s + 1) * seq, h * hd:(h + 1) * hd] = o_all[h * seq:(h + 1) * seq].astype(BF16)
    o_ref[...] = x + jnp.dot(att_scr[...], wo_ref[...], preferred_element_type=F32)


def _xattn_sample(x_new, x, g, w_q, cache_k, cache_v, w_o, layer, *, row0, n_seq, seq):
    d = x.shape[1]
    _, _, mem_len, nh, hd = cache_k.shape
    assert nh & (nh - 1) == 0
    spb = 8
    rows = spb * seq
    assert n_seq % spb == 0 and row0 % rows == 0
    blk0 = row0 // rows
    kern = functools.partial(_xattn_sample_kernel, seq=seq, spb=spb)
    return pl.pallas_call(
        kern,
        grid=(n_seq // spb,),
        in_specs=[
            pl.BlockSpec(memory_space=pl.ANY),
            pl.BlockSpec((rows, d), lambda i: (blk0 + i, 0)),
            pl.BlockSpec((None, 1, d), lambda i: (layer, 0, 0)),
            pl.BlockSpec((None, d, d), lambda i: (layer, 0, 0)),
            pl.BlockSpec((None, spb, mem_len, nh, hd), lambda i: (layer, i, 0, 0, 0)),
            pl.BlockSpec((None, spb, mem_len, nh, hd), lambda i: (layer, i, 0, 0, 0)),
            pl.BlockSpec((None, d, d), lambda i: (layer, 0, 0)),
        ],
        out_specs=pl.BlockSpec((rows, d), lambda i: (blk0 + i, 0)),
        out_shape=jax.ShapeDtypeStruct(x_new.shape, x_new.dtype),
        scratch_shapes=[pltpu.VMEM((rows, d), BF16)],
        input_output_aliases={0: 0},
        compiler_params=_params(("arbitrary",)),
        name="xattn_sample",
    )(x_new, x, g, w_q, cache_k, cache_v, w_o)


def _gate_lane_vector(v, lane0):
    depth, n = v.shape
    out = jnp.zeros((depth, 1, LANES), F32)
    return out.at[:, 0, lane0:lane0 + n].set(v.astype(F32))


def kernel(x_prompt, x_sample, mem_prompt, state_gdn, state_qkv_conv, state_short_conv, cache_mem_k, cache_mem_v, g_ffn1, w_ffn1_gu, w_ffn1_down, g_mix, w_in, conv_qkv_w, a_log, dt_bias, g_gdn_out, sconv_w, w_out, g_xattn, w_xq, w_xk, w_xv, w_xo, g_ffn2, w_ffn2_gu, w_ffn2_down, g_final):
    batch, seq, d = x_prompt.shape
    dec_batch, dec_seq, _ = x_sample.shape
    depth = w_in.shape[0]
    mem_len = mem_prompt.shape[1]
    scw = sconv_w.shape[2]
    n_p = batch * seq
    n_s = dec_batch * dec_seq
    off_beta = QKV_WIDTH + GDN_V
    off_sc = off_beta + 2 * GDN_HEADS

    w_in_p = jnp.concatenate(
        [w_in[:, :, :off_beta], w_in[:, :, off_sc:], w_in[:, :, off_beta:off_sc],
         jnp.zeros((depth, d, LANES - 2 * GDN_HEADS), w_in.dtype)], axis=2).astype(BF16)
    wgu1, wd1 = w_ffn1_gu.astype(BF16), w_ffn1_down.astype(BF16)
    wgu2, wd2 = w_ffn2_gu.astype(BF16), w_ffn2_down.astype(BF16)
    w_out_b, w_xq_b, w_xo_b = w_out.astype(BF16), w_xq.astype(BF16), w_xo.astype(BF16)
    w_xk_b, w_xv_b = w_xk.astype(BF16), w_xv.astype(BF16)
    alog_v = _gate_lane_vector(a_log, GATE_A_LANE)
    dtb_v = _gate_lane_vector(dt_bias, GATE_A_LANE)
    g1, gm, gx, g2 = (t.reshape(depth, 1, d) for t in (g_ffn1, g_mix, g_xattn, g_ffn2))
    gn = g_gdn_out.reshape(depth, 1, GDN_DV)
    gf = g_final.reshape(1, d)
    mem2 = mem_prompt.reshape(batch * mem_len, d)
    qbuf_all = jnp.pad(state_qkv_conv, ((0, 0), (0, 0), (0, dec_seq - (GDN_CONV - 1)), (0, 0)))
    cbuf_all = jnp.pad(state_short_conv, ((0, 0), (0, 0), (0, dec_seq - (SC_CONV - 1)), (0, 0)))

    p_s, p_qb, p_sb, s_qb, s_sb = [], [], [], [], []
    st_stack = k_stack = v_stack = None
    x = None
    for l in range(depth):
        xs = [x_prompt.reshape(n_p, d), x_sample.reshape(n_s, d)] if l == 0 else [x]
        x = _ffn(xs, g1, wgu1, wd1, gf, l, final_norm=False)

        x1, st_p, qb_p, chb_p = _mixer_pipe(x, gm, w_in_p, conv_qkv_w, sconv_w, alog_v, dtb_v, gn,
                                              w_out_b, l, batch=batch, seq=seq)
        x1, st_stack, up_s, ch_s = _mixer_sample(
            x1, x, gm, w_in_p, state_gdn, st_stack, qbuf_all[l].reshape(n_s, QKV_WIDTH),
            cbuf_all[l].reshape(n_s, scw), conv_qkv_w, sconv_w, alog_v, dtb_v, gn, w_out_b, l,
            row0=n_p, n_seq=dec_batch, seq=dec_seq)

        mk, mv, k_stack, v_stack = _mem_kv(mem2, w_xk_b, w_xv_b, k_stack, v_stack, l,
                                           depth=depth, batch=batch, mem_len=mem_len)
        x2 = _xattn_prompt(x1, gx, w_xq_b, mk, mv, w_xo_b, l, batch=batch, seq=seq, mem_len=mem_len)
        x2 = _xattn_sample(x2, x1, gx, w_xq_b, cache_mem_k, cache_mem_v, w_xo_b, l,
                           row0=n_p, n_seq=dec_batch, seq=dec_seq)

        if l == depth - 1:
            y_prompt, y_sample = _ffn([x2], g2, wgu2, wd2, gf, l, final_norm=True, split_rows=n_p)
        else:
            x = _ffn([x2], g2, wgu2, wd2, gf, l, final_norm=False)

        p_s.append(st_p)
        p_qb.append(qb_p[:, SUBLANES - (GDN_CONV - 1):, :])
        p_sb.append(chb_p[:, SUBLANES - (SC_CONV - 1):, :])
        s_qb.append(up_s.reshape(dec_batch, dec_seq, QKV_WIDTH)[:, dec_seq - (GDN_CONV - 1):, :])
        s_sb.append(ch_s.reshape(dec_batch, dec_seq, scw)[:, dec_seq - (SC_CONV - 1):, :])

    return (y_prompt.reshape(batch, seq, d), y_sample.reshape(dec_batch, dec_seq, d),
            jnp.stack(p_s), jnp.stack(p_qb), jnp.stack(p_sb), k_stack, v_stack,
            st_stack, jnp.stack(s_qb), jnp.stack(s_sb))
```
